```python
import jax, jax.numpy as jnp
from jax import lax
import numpy as np

D_MODEL = 1024
BATCH = 16
SEQ = 2048
DEPTH = 2
DEC_BATCH = 32
DEC_SEQ = 8
PAST_LEN = 16384
PAGE_SIZE = 128

HEAD_DIM = 64
MIX_WIDTH = D_MODEL
LRU_WIDTH = MIX_WIDTH // 4
LRU_BLOCKS = LRU_WIDTH // HEAD_DIM
CONV_W = 4
LRU_C = 8.0
FOX_WIDTH = MIX_WIDTH // 2
FOX_HEADS = FOX_WIDTH // HEAD_DIM
RET_WIDTH = MIX_WIDTH - LRU_WIDTH - FOX_WIDTH
RET_HEADS = RET_WIDTH // HEAD_DIM
RET_CHUNK = 128
Q_BLOCK = 128
ROPE_BASE = 10000.0
FORGET_BIAS_MEAN = 6.0
N_EXPERTS = 16
N_GROUPS = 4
EXPERTS_PER_GROUP = N_EXPERTS // N_GROUPS
TOP_K = 2
D_EXPERT = D_MODEL // 2
N_IN = 2 * LRU_WIDTH + 3 * FOX_WIDTH + FOX_HEADS + 4 * RET_WIDTH
EPS = 1e-6

kernel_name = 'hymba_lru_fox_retnet_moe_step'

F32 = jnp.float32


def _in_split_points():
    widths = (LRU_WIDTH, LRU_WIDTH, FOX_WIDTH, FOX_WIDTH, FOX_WIDTH, FOX_HEADS,
              RET_WIDTH, RET_WIDTH, RET_WIDTH, RET_WIDTH)
    pts, acc = [], 0
    for w in widths[:-1]:
        acc += w
        pts.append(acc)
    return pts


def rms_norm(x, w):
    xf = x.astype(F32)
    y = xf * lax.rsqrt(jnp.mean(xf * xf, axis=-1, keepdims=True) + EPS)
    return (y * w.astype(F32)).astype(x.dtype)


def rotary(x, pos):
    half = HEAD_DIM // 2
    inv = jnp.exp(-jnp.log(ROPE_BASE) * jnp.arange(half, dtype=F32) / half)
    ang = pos.astype(F32)[:, None] * inv[None, :]
    cos = jnp.cos(ang)[None, :, None, :]
    sin = jnp.sin(ang)[None, :, None, :]
    xf = x.astype(F32)
    x1, x2 = xf[..., :half], xf[..., half:]
    return jnp.concatenate([x1 * cos - x2 * sin, x1 * sin + x2 * cos], axis=-1).astype(x.dtype)


def _lru_combine(left, right):
    a1, b1 = left
    a2, b2 = right
    return a1 * a2, a2 * b1 + b2


def rglru(xb, gb, h0, buf, conv_w, conv_b, wa, ba, wx, bx, lam):
    B, T, _ = xb.shape
    xpad = jnp.concatenate([buf.astype(xb.dtype), xb], axis=1)
    xc = conv_b + sum(xpad[:, k:k + T] * conv_w[k] for k in range(CONV_W))
    xblk = xc.reshape(B, T, LRU_BLOCKS, HEAD_DIM)
    r = jax.nn.sigmoid((jnp.einsum('btnh,nhk->btnk', xblk, wa).reshape(B, T, LRU_WIDTH) + ba).astype(F32))
    i = jax.nn.sigmoid((jnp.einsum('btnh,nhk->btnk', xblk, wx).reshape(B, T, LRU_WIDTH) + bx).astype(F32))
    log_a = -LRU_C * r * jax.nn.softplus(-lam.astype(F32))
    a = jnp.exp(log_a)
    b = jnp.sqrt(-jnp.expm1(2.0 * log_a)) * i * xc.astype(F32)
    b = b.at[:, 0].add(a[:, 0] * h0.astype(F32))
    _, h = lax.associative_scan(_lru_combine, (a, b), axis=1)
    y = h.astype(xb.dtype) * jax.nn.gelu(gb)
    return y, h[:, -1].astype(xb.dtype), xpad[:, xpad.shape[1] - (CONV_W - 1):]


def fox_prompt(q, k, v, logf):
    B, T, H, d = q.shape
    scale = d ** -0.5
    cum = jnp.cumsum(logf, axis=1).transpose(0, 2, 1)
    nb = T // Q_BLOCK
    q_blocks = q.reshape(B, nb, Q_BLOCK, H, d).transpose(1, 0, 2, 3, 4)
    c_blocks = cum.reshape(B, H, nb, Q_BLOCK).transpose(2, 0, 1, 3)
    kpos = jnp.arange(T)

    def block(args):
        qi, ci, bi = args
        s = jnp.einsum('bqhd,bkhd->bhqk', qi, k, preferred_element_type=F32) * scale
        s = s + ci[..., None] - cum[:, :, None, :]
        qpos = bi * Q_BLOCK + jnp.arange(Q_BLOCK)
        s = jnp.where(kpos[None, :] <= qpos[:, None], s, -jnp.inf)
        p = jax.nn.softmax(s, axis=-1).astype(v.dtype)
        return jnp.einsum('bhqk,bkhd->bqhd', p, v)

    o = lax.map(block, (q_blocks, c_blocks, jnp.arange(nb)))
    return o.transpose(1, 0, 2, 3, 4).reshape(B, T, H * d)


def fox_decode(q, k, v, logf, k_past, v_past, logf_past):
    B, S, H, d = q.shape
    P = k_past.shape[1]
    scale = d ** -0.5
    cum = jnp.cumsum(logf, axis=1).transpose(0, 2, 1)
    lp = logf_past.astype(F32)
    suffix = (lax.cumsum(lp, axis=1, reverse=True) - lp).transpose(0, 2, 1)
    s_past = jnp.einsum('bqhd,bkhd->bhqk', q, k_past, preferred_element_type=F32) * scale
    s_past = s_past + cum[..., None] + suffix[:, :, None, :]
    s_new = jnp.einsum('bqhd,bkhd->bhqk', q, k, preferred_element_type=F32) * scale
    s_new = s_new + cum[..., None] - cum[:, :, None, :]
    causal = jnp.arange(S)[None, :] <= jnp.arange(S)[:, None]
    s_new = jnp.where(causal, s_new, -jnp.inf)
    p = jax.nn.softmax(jnp.concatenate([s_past, s_new], axis=-1), axis=-1).astype(v.dtype)
    o = (jnp.einsum('bhqk,bkhd->bqhd', p[..., :P], v_past)
         + jnp.einsum('bhqk,bkhd->bqhd', p[..., P:], v))
    return o.reshape(B, S, H * d)


def retention(q, k, v, R0, chunk):
    B, T, H, d = q.shape
    n = T // chunk
    log_g = jnp.log1p(-jnp.exp2(-5.0 - jnp.arange(H, dtype=F32)))
    j = jnp.arange(chunk, dtype=F32)
    diff = j[:, None] - j[None, :]
    dmask = jnp.where(diff >= 0, jnp.exp(log_g[:, None, None] * jnp.maximum(diff, 0.0)), 0.0)
    q_dec = jnp.exp(log_g[:, None] * (j + 1.0))[None, :, :, None]
    k_dec = jnp.exp(log_g[:, None] * (chunk - 1.0 - j))[None, :, :, None]
    c_dec = jnp.exp(log_g * chunk)[None, :, None, None]

    def to_chunks(a):
        return a.astype(F32).reshape(B, n, chunk, H, d).transpose(1, 0, 3, 2, 4)

    def step(R, inp):
        qc, kc, vc = inp
        s = jnp.einsum('bhqd,bhkd->bhqk', qc, kc) * dmask
        o = jnp.einsum('bhqk,bhkd->bhqd', s, vc) + jnp.einsum('bhqd,bhde->bhqe', qc, R) * q_dec
        R = R * c_dec + jnp.einsum('bhkd,bhke->bhde', kc * k_dec, vc)
        return R, o

    R, o = lax.scan(step, R0.astype(F32), (to_chunks(q), to_chunks(k), to_chunks(v)))
    return o.transpose(1, 0, 3, 2, 4).reshape(B, T, H, d), R


def route(t, router_w, router_bias):
    scores = jax.nn.softmax((t @ router_w).astype(F32), axis=-1)
    sel = scores + router_bias.astype(F32)
    grp = sel.reshape(-1, N_GROUPS, EXPERTS_PER_GROUP)
    grp_score = jnp.sum(lax.top_k(grp, TOP_K)[0], axis=-1)
    g = jnp.argmax(grp_score, axis=-1)
    in_grp = jnp.take_along_axis(grp, g[:, None, None], axis=1)[:, 0]
    _, local = lax.top_k(in_grp, TOP_K)
    idx = g[:, None] * EXPERTS_PER_GROUP + local
    w = jnp.take_along_axis(scores, idx, axis=-1)
    w = w / jnp.sum(w, axis=-1, keepdims=True)
    return jnp.sum(jax.nn.one_hot(idx, N_EXPERTS, dtype=F32) * w[..., None], axis=1)


def moe(h, router_w, router_bias, w_gate, w_up, w_down):
    shp = h.shape
    t = h.reshape(-1, shp[-1])
    combine = route(t, router_w, router_bias).astype(t.dtype)
    y = jnp.zeros_like(t)
    for e in range(N_EXPERTS):
        act = jax.nn.silu(t @ w_gate[e]) * (t @ w_up[e])
        y = y + (combine[:, e:e + 1] * act) @ w_down[e]
    return y.reshape(shp)


def trunk(x, c, pos, attend, lru_h0, lru_buf0, ret_R0, ret_chunk, p):
    B, T, _ = x.shape
    outs = ([], [], [], [], [], [])
    for l in range(DEPTH):
        mod = (jax.nn.silu(c) @ p['w_ada'][l] + p['b_ada'][l])[:, None, :]
        sh1, sc1, g1, sh2, sc2, g2 = jnp.split(mod, 6, axis=-1)
        h = rms_norm(x, p['norm1_w'][l]) * (1.0 + sc1) + sh1
        proj = h @ p['w_in'][l]
        xb, gb, fq, fk, fv, ff, rq, rk, rv, rg = jnp.split(proj, _in_split_points(), axis=-1)
        y_a, lru_h, lru_buf = rglru(xb, gb, lru_h0[l], lru_buf0[l], p['conv_w'][l], p['conv_b'][l],
                                    p['lru_wa'][l], p['lru_ba'][l], p['lru_wx'][l], p['lru_bx'][l],
                                    p['lru_lambda'][l])
        q = rms_norm(fq.reshape(B, T, FOX_HEADS, HEAD_DIM), p['q_norm_w'][l])
        k = rms_norm(fk.reshape(B, T, FOX_HEADS, HEAD_DIM), p['k_norm_w'][l])
        v = fv.reshape(B, T, FOX_HEADS, HEAD_DIM)
        logf = jax.nn.log_sigmoid((ff + p['forget_bias'][l]).astype(F32))
        y_b = attend(l, q, k, v, logf).astype(x.dtype)
        rq_ = rotary(rq.reshape(B, T, RET_HEADS, HEAD_DIM), pos)
        rk_ = rotary(rk.reshape(B, T, RET_HEADS, HEAD_DIM), pos) * (HEAD_DIM ** -0.5)
        o, R = retention(rq_, rk_, rv.reshape(B, T, RET_HEADS, HEAD_DIM), ret_R0[l], ret_chunk)
        mu = jnp.mean(o, axis=-1, keepdims=True)
        var = jnp.mean(jnp.square(o - mu), axis=-1, keepdims=True)
        o = (o - mu) * lax.rsqrt(var + EPS) * p['ret_norm_w'][l].reshape(RET_HEADS, HEAD_DIM).astype(F32)
        y_c = o.reshape(B, T, RET_WIDTH).astype(x.dtype) * jax.nn.silu(rg)
        mix = jnp.concatenate([y_a, y_b, y_c], axis=-1) @ p['w_out'][l]
        x = x + g1 * mix
        h2 = rms_norm(x, p['norm2_w'][l]) * (1.0 + sc2) + sh2
        x = x + g2 * moe(h2, p['router_w'], p['router_bias'], p['w_gate'][l], p['w_up'][l], p['w_down'][l])
        vals = (k, v, logf.astype(x.dtype), lru_h, lru_buf, R.astype(x.dtype))
        for lst, val in zip(outs, vals):
            lst.append(val)
    return x, [jnp.stack(lst) for lst in outs]


def setup_inputs(seed: int = 0) -> dict:
    key = jax.random.key(seed)
    ks = jax.random.split(key, 40)
    n_pages = PAST_LEN // PAGE_SIZE
    n_used = DEC_BATCH * n_pages
    n_pool = n_used + n_used // 4

    def nrm(k, shape, s=1.0):
        return jax.random.normal(k, shape, F32) * s

    a_base = jax.random.uniform(ks[20], (DEPTH, LRU_WIDTH), F32, 0.9, 0.999) ** (1.0 / LRU_C)
    lru_lambda = jnp.log(a_base) - jnp.log1p(-a_base)
    page_table = jax.random.permutation(ks[21], n_pool)[:n_used].reshape(DEC_BATCH, n_pages).astype(jnp.int32)
    return {
        'x_prompt': nrm(ks[0], (BATCH, SEQ, D_MODEL)),
        'x_sample': nrm(ks[1], (DEC_BATCH, DEC_SEQ, D_MODEL)),
        'cache_k': nrm(ks[2], (DEPTH, n_pool, PAGE_SIZE, FOX_HEADS, HEAD_DIM)),
        'cache_v': nrm(ks[3], (DEPTH, n_pool, PAGE_SIZE, FOX_HEADS, HEAD_DIM)),
        'cache_logf': jax.nn.log_sigmoid(FORGET_BIAS_MEAN + nrm(ks[4], (DEPTH, n_pool, PAGE_SIZE, FOX_HEADS), 0.5)),
        'state_lru_h': nrm(ks[5], (DEPTH, DEC_BATCH, LRU_WIDTH), 0.5),
        'state_lru_conv': nrm(ks[6], (DEPTH, DEC_BATCH, CONV_W - 1, LRU_WIDTH)),
        'state_ret': nrm(ks[7], (DEPTH, DEC_BATCH, RET_HEADS, HEAD_DIM, HEAD_DIM)),
        'page_table': page_table,
        'c_prompt': nrm(ks[8], (BATCH, D_MODEL)),
        'c_sample': nrm(ks[9], (DEC_BATCH, D_MODEL)),
        'w_in': nrm(ks[10], (DEPTH, D_MODEL, N_IN), D_MODEL ** -0.5),
        'w_out': nrm(ks[11], (DEPTH, MIX_WIDTH, D_MODEL), MIX_WIDTH ** -0.5),
        'conv_w': nrm(ks[12], (DEPTH, CONV_W, LRU_WIDTH), CONV_W ** -0.5),
        'conv_b': nrm(ks[13], (DEPTH, LRU_WIDTH), 0.02),
        'lru_wa': nrm(ks[14], (DEPTH, LRU_BLOCKS, HEAD_DIM, HEAD_DIM), HEAD_DIM ** -0.5),
        'lru_ba': nrm(ks[15], (DEPTH, LRU_WIDTH), 0.02),
        'lru_wx': nrm(ks[16], (DEPTH, LRU_BLOCKS, HEAD_DIM, HEAD_DIM), HEAD_DIM ** -0.5),
        'lru_bx': nrm(ks[17], (DEPTH, LRU_WIDTH), 0.02),
        'lru_lambda': lru_lambda,
        'q_norm_w': 1.0 + nrm(ks[18], (DEPTH, HEAD_DIM), 0.02),
        'k_norm_w': 1.0 + nrm(ks[19], (DEPTH, HEAD_DIM), 0.02),
        'forget_bias': FORGET_BIAS_MEAN + nrm(ks[22], (DEPTH, FOX_HEADS), 0.5),
        'ret_norm_w': 1.0 + nrm(ks[23], (DEPTH, RET_WIDTH), 0.02),
        'norm1_w': 1.0 + nrm(ks[24], (DEPTH, D_MODEL), 0.02),
        'norm2_w': 1.0 + nrm(ks[25], (DEPTH, D_MODEL), 0.02),
        'w_ada': nrm(ks[26], (DEPTH, D_MODEL, 6 * D_MODEL), 0.5 * D_MODEL ** -0.5),
        'b_ada': nrm(ks[27], (DEPTH, 6 * D_MODEL), 0.02),
        'router_w': nrm(ks[28], (D_MODEL, N_EXPERTS), D_MODEL ** -0.5),
        'router_bias': nrm(ks[29], (N_EXPERTS,), 0.01),
        'w_gate': nrm(ks[30], (DEPTH, N_EXPERTS, D_MODEL, D_EXPERT), D_MODEL ** -0.5),
        'w_up': nrm(ks[31], (DEPTH, N_EXPERTS, D_MODEL, D_EXPERT), D_MODEL ** -0.5),
        'w_down': nrm(ks[32], (DEPTH, N_EXPERTS, D_EXPERT, D_MODEL), D_EXPERT ** -0.5),
    }


def reference(x_prompt, x_sample, cache_k, cache_v, cache_logf, state_lru_h, state_lru_conv, state_ret,
              page_table, c_prompt, c_sample, w_in, w_out, conv_w, conv_b, lru_wa, lru_ba, lru_wx, lru_bx,
              lru_lambda, q_norm_w, k_norm_w, forget_bias, ret_norm_w, norm1_w, norm2_w, w_ada, b_ada,
              router_w, router_bias, w_gate, w_up, w_down):
    params = {'w_in': w_in, 'w_out': w_out, 'conv_w': conv_w, 'conv_b': conv_b, 'lru_wa': lru_wa,
              'lru_ba': lru_ba, 'lru_wx': lru_wx, 'lru_bx': lru_bx, 'lru_lambda': lru_lambda,
              'q_norm_w': q_norm_w, 'k_norm_w': k_norm_w, 'forget_bias': forget_bias,
              'ret_norm_w': ret_norm_w, 'norm1_w': norm1_w, 'norm2_w': norm2_w, 'w_ada': w_ada,
              'b_ada': b_ada, 'router_w': router_w, 'router_bias': router_bias, 'w_gate': w_gate,
              'w_up': w_up, 'w_down': w_down}
    bp, tp, _ = x_prompt.shape
    dt = x_prompt.dtype
    y_prompt, (kp, vp, lfp, hp, cp, rp) = trunk(
        x_prompt, c_prompt, jnp.arange(tp),
        lambda l, q, k, v, lf: fox_prompt(q, k, v, lf),
        jnp.zeros((DEPTH, bp, LRU_WIDTH), dt), jnp.zeros((DEPTH, bp, CONV_W - 1, LRU_WIDTH), dt),
        jnp.zeros((DEPTH, bp, RET_HEADS, HEAD_DIM, HEAD_DIM), dt), RET_CHUNK, params)
    n_seq, n_pages = page_table.shape
    past = n_pages * cache_k.shape[2]
    ts = x_sample.shape[1]

    def attend_sample(l, q, k, v, lf):
        k_past = cache_k[l][page_table].reshape(n_seq, past, FOX_HEADS, HEAD_DIM)
        v_past = cache_v[l][page_table].reshape(n_seq, past, FOX_HEADS, HEAD_DIM)
        lf_past = cache_logf[l][page_table].reshape(n_seq, past, FOX_HEADS)
        return fox_decode(q, k, v, lf, k_past, v_past, lf_past)

    y_sample, (ks_, vs_, lfs, hs, cs, rs) = trunk(
        x_sample, c_sample, past + jnp.arange(ts), attend_sample,
        state_lru_h, state_lru_conv, state_ret, ts, params)
    return (y_prompt, y_sample, kp, vp, lfp, hp, cp, rp, ks_, vs_, lfs, hs, cs, rs)
```

```python
import functools
import math

import jax
import jax.numpy as jnp
import numpy as np
from jax import lax
from jax.experimental import pallas as pl
from jax.experimental.pallas import tpu as pltpu

F32 = jnp.float32
BF16 = jnp.bfloat16

HEAD_DIM = 64
LRU_WIDTH = 256
LRU_BLOCKS = 4
CONV_W = 4
LRU_C = 8.0
FOX_WIDTH = 512
FOX_HEADS = 8
RET_WIDTH = 256
RET_HEADS = 4
ROPE_BASE = 10000.0
N_EXPERTS = 16
N_GROUPS = 4
EXPERTS_PER_GROUP = 4
EPS = 1e-6
LANES = 128
N_PROJ = 3200

VMEM_LIMIT = 56 * 1024 * 1024


def _cp(sem):
    return pltpu.CompilerParams(dimension_semantics=sem, vmem_limit_bytes=VMEM_LIMIT)


def _dot(a, b):
    return jnp.dot(a, b, preferred_element_type=F32)


def _dot_nt(a, b):
    return lax.dot_general(a, b, (((1,), (1,)), ((), ())), preferred_element_type=F32)


def _split2(x):
    hi = x.astype(BF16)
    lo = (x - hi.astype(F32)).astype(BF16)
    return hi, lo


def _split3(x):
    hi = x.astype(BF16)
    r = x - hi.astype(F32)
    mid = r.astype(BF16)
    lo = (r - mid.astype(F32)).astype(BF16)
    return hi, mid, lo


def _dot_f32(a, b):
    ah, al = _split2(a)
    bh, bl = _split2(b)
    return _dot(ah, bh) + (_dot(ah, bl) + _dot(al, bh))


def _dot_exact_rhs(a, m):
    hi, mid, lo = _split3(a)
    return _dot(hi, m) + (_dot(mid, m) + _dot(lo, m))


def _lhs(a, precise):
    return _split2(a.astype(F32)) if precise else (a.astype(BF16),)


def _mm(lhs, w_refs, cols=None):
    sl = (lambda r: r[...]) if cols is None else (lambda r: r[cols[0], cols[1]])
    if len(w_refs) == 1:
        return _dot(lhs[0], sl(w_refs[0]))
    wh, wl = sl(w_refs[0]), sl(w_refs[1])
    return _dot(lhs[0], wh) + (_dot(lhs[0], wl) + _dot(lhs[1], wh))


def _dx(a, b, precise, nt=False):
    f = _dot_nt if nt else _dot
    if not precise:
        return f(a.astype(BF16), b.astype(BF16))
    ah, al = _split2(a.astype(F32))
    bh, bl = _split2(b.astype(F32))
    return f(ah, bh) + (f(ah, bl) + f(al, bh))


def _idiv(x, n):
    assert n & (n - 1) == 0
    return lax.shift_right_logical(x, n.bit_length() - 1)


def _sigmoid(x):
    return 1.0 / (1.0 + jnp.exp(-x))


def _silu(x):
    return x * _sigmoid(x)


def _softplus(x):
    return jnp.maximum(x, 0.0) + jnp.log1p(jnp.exp(-jnp.abs(x)))


def _gelu_tanh(x):
    c = math.sqrt(2.0 / math.pi)
    return 0.5 * x * (1.0 + jnp.tanh(c * (x + 0.044715 * (x * x * x))))


def _rms(x, w):
    ms = jnp.mean(x * x, axis=-1, keepdims=True)
    return x * lax.rsqrt(ms + EPS) * w


def _ada_kernel(c_ref, w_ref, b_ref, o_ref):
    s = _silu(c_ref[...])
    o_ref[...] = _dot_f32(s, w_ref[...]) + b_ref[...]


def _ada(c_all, w_ada, b_ada):
    depth, d, n6 = w_ada.shape
    bc = c_all.shape[0]
    tn = 1024
    return pl.pallas_call(
        _ada_kernel,
        out_shape=jax.ShapeDtypeStruct((depth, bc, n6), F32),
        grid=(depth, n6 // tn),
        in_specs=[
            pl.BlockSpec((bc, d), lambda l, j: (0, 0)),
            pl.BlockSpec((None, d, tn), lambda l, j: (l, 0, j)),
            pl.BlockSpec((None, 1, tn), lambda l, j: (l, 0, j)),
        ],
        out_specs=pl.BlockSpec((None, bc, tn), lambda l, j: (l, 0, j)),
        compiler_params=_cp(("parallel", "parallel")),
        name="ada_mod",
    )(c_all, w_ada, b_ada.reshape(depth, 1, n6))


def _proj_kernel(has_alias, precise, x_ref, sh_ref, sc_ref, n1_ref, qw_ref, kw_ref, fb_ref, g_ref,
                 cos_ref, sin_ref, *rest):
    nw = 2 if precise else 1
    w_refs, rest = rest[:nw], rest[nw:]
    if has_alias:
        rest = rest[2:]
    xbgb_ref, q_ref, k32_ref, k16_ref, v32_ref, v16_ref, rqkv_ref, rg_ref, lf_ref = rest
    x = x_ref[...]
    h = _rms(x, n1_ref[...]) * (1.0 + sc_ref[...]) + sh_ref[...]
    hl = _lhs(h, precise)
    g = g_ref[...]

    def seg(lo, hi):
        return _mm(hl, w_refs, (slice(None), slice(lo, hi)))

    xbgb_ref[...] = seg(0, 512)

    def head_norm(t, w):
        hi, lo = _split2(t * t)
        ms = _dot(hi, g) + _dot(lo, g)
        return t * lax.rsqrt(ms + EPS) * w

    q = head_norm(seg(512, 1024), qw_ref[...])
    q_ref[...] = (q * (HEAD_DIM ** -0.5)).astype(q_ref.dtype)

    def store_layer(ref, val):
        if has_alias:
            ref[...] = val
        else:
            ref[0] = val
            ref[1:] = jnp.zeros((ref.shape[0] - 1,) + val.shape, val.dtype)

    k = head_norm(seg(1024, 1536), kw_ref[...])
    store_layer(k32_ref, k)
    k16_ref[...] = k.astype(BF16)
    v = seg(1536, 2048)
    store_layer(v32_ref, v)
    v16_ref[...] = v.astype(BF16)

    cos = cos_ref[...]
    sin = sin_ref[...]
    first_half = (lax.broadcasted_iota(jnp.int32, cos.shape, 1) & 32) == 0

    def rope(t):
        rot = jnp.where(first_half, -pltpu.roll(t, RET_WIDTH - 32, 1), pltpu.roll(t, 32, 1))
        return t * cos + rot * sin

    rdt = rqkv_ref.dtype
    rqkv_ref[:, 0:256] = rope(seg(2048, 2304)).astype(rdt)
    rqkv_ref[:, 256:512] = (rope(seg(2304, 2560)) * (HEAD_DIM ** -0.5)).astype(rdt)
    rqkv_ref[:, 512:768] = seg(2560, 2816).astype(rdt)
    rg_ref[...] = seg(2816, 3072)

    ff = seg(3072, 3200) + fb_ref[...]
    lf_ref[...] = -_softplus(-ff)


def _proj(layer, x, mod, per_token, n1, w_in_hl, qw, kw, fb, gmat, cos_t, sin_t, kv_prev, tm, precise, adt):
    b, t, d = x.shape
    depth = w_in_hl[0].shape[0]
    nt = t // tm
    tmm = tm if per_token else 1
    mi = (lambda bi, ti: ti) if per_token else (lambda bi, ti: 0)
    has_alias = kv_prev is not None
    w_list = list(w_in_hl) if precise else [w_in_hl[0]]
    in_specs = [
        pl.BlockSpec((None, tm, d), lambda bi, ti: (bi, ti, 0)),
        pl.BlockSpec((None, tmm, d), lambda bi, ti: (bi, mi(bi, ti), 0)),
        pl.BlockSpec((None, tmm, d), lambda bi, ti: (bi, mi(bi, ti), 1)),
        pl.BlockSpec((None, 1, d), lambda bi, ti: (layer, 0, 0)),
        pl.BlockSpec((None, 1, FOX_WIDTH), lambda bi, ti: (layer, 0, 0)),
        pl.BlockSpec((None, 1, FOX_WIDTH), lambda bi, ti: (layer, 0, 0)),
        pl.BlockSpec((None, 1, LANES), lambda bi, ti: (layer, 0, 0)),
        pl.BlockSpec((FOX_WIDTH, FOX_WIDTH), lambda bi, ti: (0, 0)),
        pl.BlockSpec((tm, RET_WIDTH), lambda bi, ti: (ti, 0)),
        pl.BlockSpec((tm, RET_WIDTH), lambda bi, ti: (ti, 0)),
    ] + [pl.BlockSpec((None, d, N_PROJ), lambda bi, ti: (layer, 0, 0), pipeline_mode=pl.Buffered(1)) for _ in w_list]
    args = [x, mod, mod, n1, qw, kw, fb, gmat, cos_t, sin_t] + w_list
    aliases = {}
    if has_alias:
        in_specs += [pl.BlockSpec(memory_space=pl.ANY), pl.BlockSpec(memory_space=pl.ANY)]
        aliases = {len(args): 2, len(args) + 1: 4}
        args += list(kv_prev)
    tok = lambda w, dt: jax.ShapeDtypeStruct((b, t, w), dt)
    kv_shape = jax.ShapeDtypeStruct((depth, b, t, FOX_WIDTH), F32)
    tok_spec = lambda w: pl.BlockSpec((None, tm, w), lambda bi, ti: (bi, ti, 0))
    if has_alias:
        kv_spec = pl.BlockSpec((None, None, tm, FOX_WIDTH), lambda bi, ti: (layer, bi, ti, 0))
    else:
        assert layer == 0
        kv_spec = pl.BlockSpec((depth, None, tm, FOX_WIDTH), lambda bi, ti: (0, bi, ti, 0))
    return pl.pallas_call(
        functools.partial(_proj_kernel, has_alias, precise),
        out_shape=(tok(512, F32), tok(FOX_WIDTH, adt), kv_shape, tok(FOX_WIDTH, BF16), kv_shape,
                   tok(FOX_WIDTH, BF16), tok(768, adt), tok(RET_WIDTH, F32), tok(LANES, F32)),
        grid=(b, nt),
        in_specs=in_specs,
        out_specs=(tok_spec(512), tok_spec(FOX_WIDTH), kv_spec, tok_spec(FOX_WIDTH), kv_spec,
                   tok_spec(FOX_WIDTH), tok_spec(768), tok_spec(RET_WIDTH), tok_spec(LANES)),
        input_output_aliases=aliases,
        compiler_params=_cp(("parallel", "parallel")),
        name="norm_proj",
    )(*args)


def _lru_kernel(ch, precise, x_ref, h0_ref, buf_ref, cw_ref, cb_ref, ba_ref, bx_ref, lam_ref, *rest):
    nw = 2 if precise else 1
    wa_refs, wx_refs, rest = rest[:nw], rest[nw:2 * nw], rest[2 * nw:]
    y_ref, hl_ref, bo_ref, xpad, a_s, b_s, h_s, hc = rest
    ci = pl.program_id(1)

    @pl.when(ci == 0)
    def _():
        xpad[0:8, :] = jnp.zeros((8, LRU_WIDTH), F32)
        xpad[5:8, :] = buf_ref[...]
        hc[...] = jnp.broadcast_to(h0_ref[...], (8, LRU_WIDTH))

    xb = x_ref[:, 0:LRU_WIDTH]
    gb = x_ref[:, LRU_WIDTH:2 * LRU_WIDTH]
    xpad[8:8 + ch, :] = xb
    xc = cw_ref[0:1, :] * xpad[5:5 + ch, :]
    xc = xc + cw_ref[1:2, :] * xpad[6:6 + ch, :]
    xc = xc + cw_ref[2:3, :] * xpad[7:7 + ch, :]
    xc = xc + cw_ref[3:4, :] * xb
    xc = xc + cb_ref[...]
    xl = _lhs(xc, precise)
    r = _sigmoid(_mm(xl, wa_refs) + ba_ref[...])
    i = _sigmoid(_mm(xl, wx_refs) + bx_ref[...])
    log_a = (-LRU_C) * r * _softplus(-lam_ref[...])
    a_s[...] = jnp.exp(log_a)
    th = jnp.tanh(log_a)
    b_s[...] = jnp.sqrt(-2.0 * th / (1.0 - th)) * i * xc

    row = lax.broadcasted_iota(jnp.int32, (8, LRU_WIDTH), 0)

    def blk(j, h):
        r0 = pl.multiple_of(j * 8, 8)
        a = a_s[pl.ds(r0, 8), :]
        bv = b_s[pl.ds(r0, 8), :]
        for s in (1, 2, 4):
            keep = row >= s
            a_sh = jnp.where(keep, pltpu.roll(a, s, 0), 1.0)
            b_sh = jnp.where(keep, pltpu.roll(bv, s, 0), 0.0)
            bv = a * b_sh + bv
            a = a * a_sh
        hb = a * h + bv
        h_s[pl.ds(r0, 8), :] = hb
        return jnp.broadcast_to(hb[7:8, :], (8, LRU_WIDTH))

    h_end = lax.fori_loop(0, ch // 8, blk, hc[...])
    hc[...] = h_end
    y_ref[...] = (h_s[...] * _gelu_tanh(gb)).astype(y_ref.dtype)
    tail = xpad[ch:ch + 8, :]
    xpad[0:8, :] = tail

    @pl.when(ci == pl.num_programs(1) - 1)
    def _():
        hl_ref[...] = h_end[0:1, :]
        bo_ref[...] = tail[5:8, :]


def _lru(xbgb, h0, buf, cw, cb, wa_hl, ba, wx_hl, bx, lam, precise, ydt):
    b, t, _ = xbgb.shape
    ch = min(t, 256)
    w = LRU_WIDTH
    vec = lambda: pl.BlockSpec((1, w), lambda bi, ci: (0, 0))
    mats = (list(wa_hl) + list(wx_hl)) if precise else [wa_hl[0], wx_hl[0]]
    return pl.pallas_call(
        functools.partial(_lru_kernel, ch, precise),
        out_shape=(jax.ShapeDtypeStruct((b, t, w), ydt), jax.ShapeDtypeStruct((b, 1, w), F32),
                   jax.ShapeDtypeStruct((b, CONV_W - 1, w), F32)),
        grid=(b, t // ch),
        in_specs=[
            pl.BlockSpec((None, ch, 2 * w), lambda bi, ci: (bi, ci, 0)),
            pl.BlockSpec((None, 1, w), lambda bi, ci: (bi, 0, 0)),
            pl.BlockSpec((None, CONV_W - 1, w), lambda bi, ci: (bi, 0, 0)),
            pl.BlockSpec((CONV_W, w), lambda bi, ci: (0, 0)),
            vec(), vec(), vec(), vec(),
        ] + [pl.BlockSpec((w, w), lambda bi, ci: (0, 0)) for _ in mats],
        out_specs=(pl.BlockSpec((None, ch, w), lambda bi, ci: (bi, ci, 0)),
                   pl.BlockSpec((None, 1, w), lambda bi, ci: (bi, 0, 0)),
                   pl.BlockSpec((None, CONV_W - 1, w), lambda bi, ci: (bi, 0, 0))),
        scratch_shapes=[pltpu.VMEM((ch + 8, w), F32), pltpu.VMEM((ch, w), F32), pltpu.VMEM((ch, w), F32),
                        pltpu.VMEM((ch, w), F32), pltpu.VMEM((8, w), F32)],
        compiler_params=_cp(("parallel", "arbitrary")),
        name="rglru",
    )(xbgb, h0, buf, cw, cb, ba, bx, lam, *mats)


def _cum_kernel(t, tc, lf_ref, cum_ref, cumt_ref, carry):
    ci = pl.program_id(1)

    @pl.when(ci == 0)
    def _():
        carry[...] = jnp.zeros_like(carry)

    lf = lf_ref[...]
    if t < tc:
        lf = jnp.concatenate([lf, jnp.zeros((tc - t, LANES), F32)], axis=0)
    r = lax.broadcasted_iota(jnp.int32, (tc, tc), 0)
    c = lax.broadcasted_iota(jnp.int32, (tc, tc), 1)
    tri = jnp.where(c <= r, 1.0, 0.0).astype(BF16)
    hi, mid, lo = _split3(lf)
    cum = _dot(tri, hi) + (_dot(tri, mid) + _dot(tri, lo)) + carry[0:1, :]
    carry[...] = jnp.broadcast_to(cum[tc - 1:tc, :], carry.shape)
    cum_ref[...] = cum[0:t, :] if t < tc else cum
    cumt_ref[...] = cum.T[0:FOX_HEADS, :]


def _cum(lf):
    b, t, _ = lf.shape
    tc = 128 if t < 128 else min(t, 512)
    tb = min(t, tc)
    nt = max(t // tc, 1)
    return pl.pallas_call(
        functools.partial(_cum_kernel, tb, tc),
        out_shape=(jax.ShapeDtypeStruct((b, t, LANES), F32), jax.ShapeDtypeStruct((b, FOX_HEADS, nt * tc), F32)),
        grid=(b, nt),
        in_specs=[pl.BlockSpec((None, tb, LANES), lambda bi, ci: (bi, ci, 0))],
        out_specs=(pl.BlockSpec((None, tb, LANES), lambda bi, ci: (bi, ci, 0)),
                   pl.BlockSpec((None, FOX_HEADS, tc), lambda bi, ci: (bi, 0, ci))),
        scratch_shapes=[pltpu.VMEM((8, LANES), F32)],
        compiler_params=_cp(("parallel", "arbitrary")),
        name="logf_cumsum",
    )(lf)


def _fox_kernel(tq, q_ref, k_ref, v_ref, cum_ref, cumt_ref, o_ref):
    hp = pl.program_id(1)
    qi = pl.program_id(2)
    q = q_ref[...]
    cum = cum_ref[...]
    lane = lax.broadcasted_iota(jnp.int32, (tq, LANES), 1)
    low = lane < HEAD_DIM
    rr = lax.broadcasted_iota(jnp.int32, (tq, tq), 0)
    cc = lax.broadcasted_iota(jnp.int32, (tq, tq), 1)
    causal = cc <= rr
    zero = jnp.zeros_like(q)
    outs = []
    for hh in range(2):
        head = 2 * hp + hh
        qm = jnp.where(low if hh == 0 else jnp.logical_not(low), q, zero)
        cq = jnp.sum(jnp.where(lane == head, cum, 0.0), axis=-1, keepdims=True)

        def scores(ki):
            k0 = pl.multiple_of(ki * tq, tq)
            s = _dot_nt(qm, k_ref[pl.ds(k0, tq), :])
            ck = cumt_ref[pl.ds(head, 1), pl.ds(k0, tq)]
            return s + (cq - ck), k0

        def update(s, k0, carry):
            m, l, acc = carry
            m_new = jnp.maximum(m, jnp.max(s, axis=-1, keepdims=True))
            alpha = jnp.exp(m - m_new)
            p = jnp.exp(s - m_new)
            l = alpha * l + jnp.sum(p, axis=-1, keepdims=True)
            acc = alpha * acc + _dot(p.astype(BF16), v_ref[pl.ds(k0, tq), :])
            return m_new, l, acc

        def body(ki, carry):
            s, k0 = scores(ki)
            return update(s, k0, carry)

        init = (jnp.full((tq, 1), -jnp.inf, F32), jnp.zeros((tq, 1), F32), jnp.zeros((tq, LANES), F32))
        carry = lax.fori_loop(0, qi, body, init)
        s, k0 = scores(qi)
        m, l, acc = update(jnp.where(causal, s, -jnp.inf), k0, carry)
        outs.append(acc / l)
    o_ref[...] = jnp.where(low, outs[0], outs[1]).astype(o_ref.dtype)


def _fox_prompt(q16, k16, v16, cum, cumt, tq, ydt):
    b, t, _ = q16.shape
    npair = FOX_HEADS // 2
    return pl.pallas_call(
        functools.partial(_fox_kernel, tq),
        out_shape=jax.ShapeDtypeStruct((b, t, FOX_WIDTH), ydt),
        grid=(b, npair, t // tq),
        in_specs=[
            pl.BlockSpec((None, tq, LANES), lambda bi, hp, qi: (bi, qi, hp)),
            pl.BlockSpec((None, t, LANES), lambda bi, hp, qi: (bi, 0, hp)),
            pl.BlockSpec((None, t, LANES), lambda bi, hp, qi: (bi, 0, hp)),
            pl.BlockSpec((None, tq, LANES), lambda bi, hp, qi: (bi, qi, 0)),
            pl.BlockSpec((None, FOX_HEADS, t), lambda bi, hp, qi: (bi, 0, 0)),
        ],
        out_specs=pl.BlockSpec((None, tq, LANES), lambda bi, hp, qi: (bi, qi, hp)),
        compiler_params=_cp(("parallel", "parallel", "arbitrary")),
        name="fox_prompt",
    )(q16, k16, v16, cum, cumt)


def _suffix_kernel(layer, n_pages, pt_ref, lf_hbm, md_ref, ssum_ref, o_ref, xbuf, sem):
    bi = pl.program_id(0)

    def copy(p):
        return pltpu.make_async_copy(lf_hbm.at[layer, pt_ref[bi, p]], xbuf.at[pl.ds(p, 1)], sem)

    def start(p, c):
        copy(p).start()
        return c

    def wait(p, c):
        copy(p).wait()
        return c

    lax.fori_loop(0, n_pages, start, 0)
    lax.fori_loop(0, n_pages, wait, 0)
    x = xbuf[...]
    within = _dot_exact_rhs(x, md_ref[...])
    tot = _dot_exact_rhs(x, ssum_ref[...])
    r = lax.broadcasted_iota(jnp.int32, (n_pages, n_pages), 0)
    c = lax.broadcasted_iota(jnp.int32, (n_pages, n_pages), 1)
    later = jnp.where(c > r, 1.0, 0.0).astype(BF16)
    hi, mid, lo = _split3(tot)
    o_ref[...] = within + (_dot(later, hi) + (_dot(later, mid) + _dot(later, lo)))


def _suffix_bias(layer, cache_logf, page_table):
    depth, n_pool, ps, nh = cache_logf.shape
    b, n_pages = page_table.shape
    w = ps * nh
    lf = cache_logf.reshape(depth, n_pool, 1, w)
    idx = np.arange(w)
    pos, head = idx // nh, idx % nh
    same = head[:, None] == head[None, :]
    md = jnp.asarray(same & (pos[:, None] > pos[None, :]), BF16)
    ssum = jnp.asarray(same, BF16)
    grid_spec = pltpu.PrefetchScalarGridSpec(
        num_scalar_prefetch=1,
        grid=(b,),
        in_specs=[
            pl.BlockSpec(memory_space=pl.ANY),
            pl.BlockSpec((w, w), lambda bi, pt: (0, 0)),
            pl.BlockSpec((w, w), lambda bi, pt: (0, 0)),
        ],
        out_specs=pl.BlockSpec((None, n_pages, w), lambda bi, pt: (bi, 0, 0)),
        scratch_shapes=[pltpu.VMEM((n_pages, w), F32), pltpu.SemaphoreType.DMA(())],
    )
    return pl.pallas_call(
        functools.partial(_suffix_kernel, layer, n_pages),
        out_shape=jax.ShapeDtypeStruct((b, n_pages, w), F32),
        grid_spec=grid_spec,
        compiler_params=_cp(("arbitrary",)),
        name="logf_suffix",
    )(page_table, lf, md, ssum)


def _decode_kernel(pp, nh, ps, s_new, pt_ref, q_ref, cb_ref, bias_ref, kn_ref, vn_ref, nb_ref, *rest):
    k_refs = rest[:pp]
    v_refs = rest[pp:2 * pp]
    o_ref, m_s, l_s, acc_s = rest[2 * pp:]
    j = pl.program_id(1)
    rows = nh * s_new
    cols = ps * nh

    @pl.when(j == 0)
    def _():
        m_s[...] = jnp.full_like(m_s, -jnp.inf)
        l_s[...] = jnp.zeros_like(l_s)
        acc_s[...] = jnp.zeros_like(acc_s)

    q = q_ref[...]
    cb = cb_ref[...]
    row_head = _idiv(lax.broadcasted_iota(jnp.int32, (rows, cols), 0), s_new)
    col = lax.broadcasted_iota(jnp.int32, (rows, cols), 1)
    same_head = (col & (nh - 1)) == row_head

    def qk(kb):
        s2 = _dot_nt(q, kb)
        return s2[0:rows, :] + s2[rows:2 * rows, :]

    def step(s_list, v_list):
        m = m_s[...]
        m_new = m
        for s in s_list:
            m_new = jnp.maximum(m_new, jnp.max(s, axis=-1, keepdims=True))
        alpha = jnp.exp(m - m_new)
        l = alpha * l_s[...]
        acc = alpha * acc_s[...]
        for s, v in zip(s_list, v_list):
            p = jnp.exp(s - m_new)
            l = l + jnp.sum(p, axis=-1, keepdims=True)
            acc = acc + _dot(p.astype(BF16), v)
        m_s[...] = m_new
        l_s[...] = l
        acc_s[...] = acc

    s_list, v_list = [], []
    for r in range(pp):
        kb = k_refs[r][...].reshape(cols, HEAD_DIM).astype(BF16)
        s = qk(kb) + (bias_ref[r:r + 1, :] + cb)
        s_list.append(jnp.where(same_head, s, -jnp.inf))
        v_list.append(v_refs[r][...].reshape(cols, HEAD_DIM).astype(BF16))
    step(s_list, v_list)

    @pl.when(j == pl.num_programs(1) - 1)
    def _():
        row_tok = lax.broadcasted_iota(jnp.int32, (rows, cols), 0) & (s_new - 1)
        ok = same_head & (_idiv(col, nh) <= row_tok)
        s = qk(kn_ref[...]) + (cb - nb_ref[...])
        step([jnp.where(ok, s, -jnp.inf)], [vn_ref[...]])
        o_ref[...] = (acc_s[...] / l_s[...]).astype(o_ref.dtype)


def _fox_decode(layer, q32, k16, v16, cum, cache_k, cache_v, bias, page_table, pp, ydt):
    depth, n_pool, ps, nh, hd = cache_k.shape
    b, s_new, w = q32.shape
    n_pages = page_table.shape[1]
    rows = nh * s_new
    cols = ps * nh
    assert s_new & (s_new - 1) == 0 and nh & (nh - 1) == 0 and s_new <= ps
    qf = q32.reshape(b, s_new, nh, hd).transpose(0, 2, 1, 3).reshape(b, rows, hd)
    qf_hi = qf.astype(BF16)
    qf = jnp.concatenate([qf_hi, (qf - qf_hi.astype(F32)).astype(BF16)], axis=1)
    colb = cum[:, :, :nh].transpose(0, 2, 1).reshape(b, rows, 1)
    padn = lambda a: jnp.pad(a.reshape(b, s_new * nh, -1), ((0, 0), (0, cols - s_new * nh), (0, 0)))
    kn = padn(k16)
    vn = padn(v16)
    newb = jnp.pad(cum[:, :, :nh].reshape(b, 1, s_new * nh), ((0, 0), (0, 0), (0, cols - s_new * nh)))

    def page_spec(r):
        return pl.BlockSpec((None, None, ps, nh, hd), lambda bi, j, pt: (layer, pt[bi, j * pp + r], 0, 0, 0))

    grid_spec = pltpu.PrefetchScalarGridSpec(
        num_scalar_prefetch=1,
        grid=(b, n_pages // pp),
        in_specs=[
            pl.BlockSpec((None, 2 * rows, hd), lambda bi, j, pt: (bi, 0, 0)),
            pl.BlockSpec((None, rows, 1), lambda bi, j, pt: (bi, 0, 0)),
            pl.BlockSpec((None, pp, cols), lambda bi, j, pt: (bi, j, 0)),
            pl.BlockSpec((None, cols, hd), lambda bi, j, pt: (bi, 0, 0)),
            pl.BlockSpec((None, cols, hd), lambda bi, j, pt: (bi, 0, 0)),
            pl.BlockSpec((None, 1, cols), lambda bi, j, pt: (bi, 0, 0)),
        ] + [page_spec(r) for r in range(pp)] + [page_spec(r) for r in range(pp)],
        out_specs=pl.BlockSpec((None, rows, hd), lambda bi, j, pt: (bi, 0, 0)),
        scratch_shapes=[pltpu.VMEM((rows, 1), F32), pltpu.VMEM((rows, 1), F32), pltpu.VMEM((rows, hd), F32)],
    )
    out = pl.pallas_call(
        functools.partial(_decode_kernel, pp, nh, ps, s_new),
        out_shape=jax.ShapeDtypeStruct((b, rows, hd), ydt),
        grid_spec=grid_spec,
        compiler_params=_cp(("parallel", "arbitrary")),
        name="fox_decode",
    )(page_table, qf, colb, bias, kn, vn, newb, *([cache_k] * pp), *([cache_v] * pp))
    return out.reshape(b, nh, s_new, hd).transpose(0, 2, 1, 3).reshape(b, s_new, w)


def _ret_kernel(c, n_chunks, precise, q_ref, k_ref, v_ref, rg_ref, r0_ref, dm_ref, qd_ref, kd_ref, cd_ref, nw_ref, g_ref,
                y_ref, ro_ref, r_s):
    cp = max(c, LANES)
    lane = lax.broadcasted_iota(jnp.int32, (c, LANES), 1)
    low = lane < HEAD_DIM
    br = _idiv(lax.broadcasted_iota(jnp.int32, (LANES, LANES), 0), HEAD_DIM)
    bc = _idiv(lax.broadcasted_iota(jnp.int32, (LANES, LANES), 1), HEAD_DIM)
    blockdiag = br == bc
    g = g_ref[...]
    r_s[...] = r0_ref[...]

    def gmean(t):
        hi, lo = _split2(t)
        return _dot(hi, g) + _dot(lo, g)

    def chunk(ci, carry):
        r0 = ci * c if isinstance(ci, int) else pl.multiple_of(ci * c, c)
        q = q_ref[pl.ds(r0, c), :]
        k = k_ref[pl.ds(r0, c), :].astype(F32)
        v = v_ref[pl.ds(r0, c), :].astype(F32)
        if c < cp:
            pad = jnp.zeros((cp - c, LANES), F32)
            k = jnp.concatenate([k, pad], axis=0)
            v = jnp.concatenate([v, pad], axis=0)
        if not precise:
            v = v.astype(BF16)
        rmat = r_s[...]
        zero = jnp.zeros_like(q)
        o = _dx(q, rmat, precise) * qd_ref[...]
        for hh in range(2):
            sel = low if hh == 0 else jnp.logical_not(low)
            s = _dx(jnp.where(sel, q, zero), k, precise, nt=True) * dm_ref[hh]
            oh = _dx(s, v, precise)
            o = o + jnp.where(sel, oh, 0.0)
        rn = _dx((k * kd_ref[...]).T, v, precise)
        r_s[...] = rmat * cd_ref[...] + jnp.where(blockdiag, rn, 0.0)
        mu = gmean(o)
        oc = o - mu
        var = gmean(oc * oc)
        on = oc * lax.rsqrt(var + EPS) * nw_ref[...]
        y_ref[pl.ds(r0, c), :] = (on * _silu(rg_ref[pl.ds(r0, c), :])).astype(y_ref.dtype)
        return carry

    if n_chunks == 1:
        chunk(0, 0)
    else:
        lax.fori_loop(0, n_chunks, chunk, 0)
    ro_ref[...] = r_s[...]


def _ret_tables(c):
    cp = max(c, LANES)
    hidx = np.arange(RET_HEADS, dtype=np.float64)
    log_g = np.log1p(-np.exp2(-5.0 - hidx))
    j = np.arange(c, dtype=np.float64)
    diff = j[:, None] - j[None, :]
    dmask = np.where(diff >= 0, np.exp(log_g[:, None, None] * np.maximum(diff, 0.0)), 0.0)
    dm = np.zeros((RET_HEADS, c, cp))
    dm[:, :, :c] = dmask
    q_dec = np.exp(log_g[:, None] * (j + 1.0))
    k_dec = np.exp(log_g[:, None] * (c - 1.0 - j))
    c_dec = np.exp(log_g * c)
    npair = RET_HEADS // 2
    qd = np.zeros((npair, c, LANES))
    kd = np.zeros((npair, cp, LANES))
    cd = np.zeros((npair, 1, LANES))
    for p in range(npair):
        for hh in range(2):
            sl = slice(hh * HEAD_DIM, (hh + 1) * HEAD_DIM)
            qd[p, :, sl] = q_dec[2 * p + hh][:, None]
            kd[p, :c, sl] = k_dec[2 * p + hh][:, None]
            cd[p, 0, sl] = c_dec[2 * p + hh]
    f = lambda a: jnp.asarray(a, F32)
    return f(dm.reshape(npair, 2, c, cp)), f(qd), f(kd), f(cd)


def _retention(rqkv, rg, r0, nw, gpair, c, precise, ydt):
    b, t, _ = rg.shape
    npair = RET_HEADS // 2
    cp = max(c, LANES)
    dm, qd, kd, cd = _ret_tables(c)
    col = lambda off: pl.BlockSpec((None, t, LANES), lambda bi, p: (bi, 0, off + p))
    return pl.pallas_call(
        functools.partial(_ret_kernel, c, t // c, precise),
        out_shape=(jax.ShapeDtypeStruct((b, t, RET_WIDTH), ydt),
                   jax.ShapeDtypeStruct((b, npair, LANES, LANES), F32)),
        grid=(b, npair),
        in_specs=[
            col(0), col(npair), col(2 * npair),
            pl.BlockSpec((None, t, LANES), lambda bi, p: (bi, 0, p)),
            pl.BlockSpec((None, None, LANES, LANES), lambda bi, p: (bi, p, 0, 0)),
            pl.BlockSpec((None, 2, c, cp), lambda bi, p: (p, 0, 0, 0)),
            pl.BlockSpec((None, c, LANES), lambda bi, p: (p, 0, 0)),
            pl.BlockSpec((None, cp, LANES), lambda bi, p: (p, 0, 0)),
            pl.BlockSpec((None, 1, LANES), lambda bi, p: (p, 0, 0)),
            pl.BlockSpec((1, LANES), lambda bi, p: (0, p)),
            pl.BlockSpec((LANES, LANES), lambda bi, p: (0, 0)),
        ],
        out_specs=(pl.BlockSpec((None, t, LANES), lambda bi, p: (bi, 0, p)),
                   pl.BlockSpec((None, None, LANES, LANES), lambda bi, p: (bi, p, 0, 0))),
        scratch_shapes=[pltpu.VMEM((LANES, LANES), F32)],
        compiler_params=_cp(("parallel", "parallel")),
        name="retention",
    )(rqkv, rqkv, rqkv, rg, r0, dm, qd, kd, cd, nw, gpair)


def _route_rows(logits, bias):
    rows = [logits[e:e + 1, :] for e in range(N_EXPERTS)]
    m = rows[0]
    for r in rows[1:]:
        m = jnp.maximum(m, r)
    ex = [jnp.exp(r - m) for r in rows]
    z = ex[0]
    for e in ex[1:]:
        z = z + e
    score = [e / z for e in ex]
    sel = [score[e] + bias[e:e + 1, :] for e in range(N_EXPERTS)]
    top2, gscore = [], []
    for g in range(N_GROUPS):
        v = sel[g * EXPERTS_PER_GROUP:(g + 1) * EXPERTS_PER_GROUP]
        gs = None
        for i in range(EXPERTS_PER_GROUP):
            rank = None
            for j in range(EXPERTS_PER_GROUP):
                if j == i:
                    continue
                ahead = (v[j] >= v[i]) if j < i else (v[j] > v[i])
                ahead = jnp.where(ahead, 1.0, 0.0)
                rank = ahead if rank is None else rank + ahead
            keep = rank < 1.5
            top2.append(keep)
            term = jnp.where(keep, v[i], 0.0)
            gs = term if gs is None else gs + term
        gscore.append(gs)
    out = []
    for g in range(N_GROUPS):
        win = None
        for j in range(N_GROUPS):
            if j == g:
                continue
            c = (gscore[g] > gscore[j]) if j < g else (gscore[g] >= gscore[j])
            win = c if win is None else (win & c)
        for i in range(EXPERTS_PER_GROUP):
            e = g * EXPERTS_PER_GROUP + i
            out.append(jnp.where(win & top2[e], score[e], 0.0))
    tot = out[0]
    for o in out[1:]:
        tot = tot + o
    return jnp.concatenate([o / tot for o in out], axis=0)


def _out_kernel(precise, ya_ref, yb_ref, yc_ref, x_ref, g1_ref, sh_ref, sc_ref, n2_ref, rwh_ref, rwl_ref, rb_ref,
                *rest):
    nw = 2 if precise else 1
    w_refs, (x1_ref, h2_ref, cmb_ref) = rest[:nw], rest[nw:]
    mix = _mm(_lhs(ya_ref[...], precise), w_refs, (slice(0, 256), slice(None)))
    mix = mix + _mm(_lhs(yb_ref[...], precise), w_refs, (slice(256, 768), slice(None)))
    mix = mix + _mm(_lhs(yc_ref[...], precise), w_refs, (slice(768, 1024), slice(None)))
    x1 = x_ref[...] + g1_ref[...] * mix
    x1_ref[...] = x1
    h2 = _rms(x1, n2_ref[...]) * (1.0 + sc_ref[...]) + sh_ref[...]
    hi, lo = _split2(h2)
    h2_ref[...] = hi
    rwh = rwh_ref[...]
    logits = _dot_nt(rwh, hi) + (_dot_nt(rwh, lo) + _dot_nt(rwl_ref[...], hi))
    cmb_ref[...] = _route_rows(logits, rb_ref[...])


def _out_proj(layer, ya, yb, yc, x, mod, per_token, w_out_hl, n2, rwh, rwl, rb, tm, precise):
    b, t, d = x.shape
    tmm = tm if per_token else 1
    mi = (lambda bi, ti: ti) if per_token else (lambda bi, ti: 0)
    tok = lambda w: pl.BlockSpec((None, tm, w), lambda bi, ti: (bi, ti, 0))
    modspec = lambda col: pl.BlockSpec((None, tmm, d), lambda bi, ti: (bi, mi(bi, ti), col))
    w_list = list(w_out_hl) if precise else [w_out_hl[0]]
    return pl.pallas_call(
        functools.partial(_out_kernel, precise),
        out_shape=(jax.ShapeDtypeStruct((b, t, d), F32), jax.ShapeDtypeStruct((b, t, d), BF16),
                   jax.ShapeDtypeStruct((b, N_EXPERTS, t), F32)),
        grid=(b, t // tm),
        in_specs=[
            tok(LRU_WIDTH), tok(FOX_WIDTH), tok(RET_WIDTH), tok(d),
            modspec(2), modspec(3), modspec(4),
            pl.BlockSpec((None, 1, d), lambda bi, ti: (layer, 0, 0)),
            pl.BlockSpec((N_EXPERTS, d), lambda bi, ti: (0, 0)),
            pl.BlockSpec((N_EXPERTS, d), lambda bi, ti: (0, 0)),
            pl.BlockSpec((N_EXPERTS, 1), lambda bi, ti: (0, 0)),
        ] + [pl.BlockSpec((None, d, d), lambda bi, ti: (layer, 0, 0)) for _ in w_list],
        out_specs=(tok(d), tok(d), pl.BlockSpec((None, N_EXPERTS, tm), lambda bi, ti: (bi, 0, ti))),
        compiler_params=_cp(("parallel", "parallel")),
        name="out_proj_route",
    )(ya, yb, yc, x, mod, mod, mod, n2, rwh, rwl, rb, *w_list)


def _moe_kernel(h_ref, c_ref, x1_ref, g2_ref, wg_ref, wu_ref, wd_ref, o_ref, acc):
    e = pl.program_id(2)

    @pl.when(e == 0)
    def _():
        acc[...] = jnp.zeros_like(acc)

    h = h_ref[...]
    act = _silu(_dot(h, wg_ref[...])) * _dot(h, wu_ref[...])
    cmb = c_ref[...]
    lane = lax.broadcasted_iota(jnp.int32, cmb.shape, 1)
    ce = jnp.sum(jnp.where(lane == e, cmb, 0.0), axis=-1, keepdims=True)
    acc[...] += _dot((ce * act).astype(BF16), wd_ref[...])

    @pl.when(e == pl.num_programs(2) - 1)
    def _():
        o_ref[...] = x1_ref[...] + g2_ref[...] * acc[...]


def _moe(layer, h2, cmb, x1, mod, per_token, wg, wu, wd, tm):
    b, t, d = x1.shape
    de = wg.shape[-1]
    tmm = tm if per_token else 1
    mi = (lambda bi, ti: ti) if per_token else (lambda bi, ti: 0)
    tok = lambda w: pl.BlockSpec((None, tm, w), lambda bi, ti, e: (bi, ti, 0))
    return pl.pallas_call(
        _moe_kernel,
        out_shape=jax.ShapeDtypeStruct((b, t, d), F32),
        grid=(b, t // tm, N_EXPERTS),
        in_specs=[
            tok(d), tok(N_EXPERTS), tok(d),
            pl.BlockSpec((None, tmm, d), lambda bi, ti, e: (bi, mi(bi, ti), 5)),
            pl.BlockSpec((None, None, d, de), lambda bi, ti, e: (layer, e, 0, 0)),
            pl.BlockSpec((None, None, d, de), lambda bi, ti, e: (layer, e, 0, 0)),
            pl.BlockSpec((None, None, de, d), lambda bi, ti, e: (layer, e, 0, 0)),
        ],
        out_specs=tok(d),
        scratch_shapes=[pltpu.VMEM((tm, d), F32)],
        compiler_params=_cp(("parallel", "parallel", "arbitrary")),
        name="moe_dense",
    )(h2, cmb, x1, mod, wg, wu, wd)


def _rope_tables(pos):
    half = HEAD_DIM // 2
    inv = np.exp(-np.log(ROPE_BASE) * np.arange(half, dtype=np.float64) / half)
    ang = np.asarray(pos, np.float64)[:, None] * inv[None, :]
    reps = RET_WIDTH // half
    return (jnp.asarray(np.tile(np.cos(ang), (1, reps)), F32), jnp.asarray(np.tile(np.sin(ang), (1, reps)), F32))


def _block_diag(w):
    depth, nb, h, _ = w.shape
    eye = jnp.eye(nb, dtype=w.dtype)
    return (w[:, :, :, None, :] * eye[None, :, None, :, None]).reshape(depth, nb * h, nb * h)


def _group_mean_matrix(width):
    idx = np.arange(width) // HEAD_DIM
    return jnp.asarray((idx[:, None] == idx[None, :]) / HEAD_DIM, BF16)


def _trunk(x, mod, per_token, seq_shape, pos, lru_h0, lru_buf0, ret_r0, ret_chunk, attend, prm, tm, tm_moe, precise):
    bv, tv, d = x.shape
    b, t = seq_shape
    depth = prm['w_in_hl'][0].shape[0]
    cos_t, sin_t = _rope_tables(pos)
    kv = None
    logfs, lru_hs, lru_bufs, rets = [], [], [], []
    npair = RET_HEADS // 2
    for l in range(depth):
        m = mod[l]
        xbgb, q16, k32, k16, v32, v16, rqkv, rg, lf = _proj(
            l, x, m, per_token, prm['norm1_w'], prm['w_in_hl'], prm['q_norm_w'], prm['k_norm_w'], prm['forget_bias'],
            prm['g512'], cos_t, sin_t, kv, tm, precise or l < depth - 1, F32 if precise else BF16)
        kv = (k32, v32)
        seq = lambda a: a.reshape(b, t, a.shape[-1])
        pick = lambda hl: tuple(w[l] for w in hl)
        ya, lru_h, lru_buf = _lru(seq(xbgb), lru_h0[l], lru_buf0[l], prm['conv_w'][l], prm['conv_b'][l],
                                  pick(prm['lru_wa_hl']), prm['lru_ba'][l], pick(prm['lru_wx_hl']), prm['lru_bx'][l],
                                  prm['lru_lambda'][l], precise, F32)
        lf_s = seq(lf)
        cum, cumt = _cum(lf_s)
        yb = attend(l, seq(q16), seq(k16), seq(v16), cum, cumt)
        yc, r_out = _retention(seq(rqkv), seq(rg), ret_r0[l], prm['ret_norm_w'][l], prm['g128'], ret_chunk, precise, F32)
        tokv = lambda a: a.reshape(bv, tv, a.shape[-1])
        x1, h2, cmb_t = _out_proj(l, tokv(ya), tokv(yb), tokv(yc), x, m, per_token, prm['w_out_hl'], prm['norm2_w'],
                                  prm['rw_hi'], prm['rw_lo'], prm['router_bias'], tm, True)
        cmb = cmb_t.transpose(0, 2, 1)
        x = _moe(l, h2, cmb, x1, m, per_token, prm['w_gate16'], prm['w_up16'], prm['w_down16'], tm_moe)
        logfs.append(lf_s[:, :, :FOX_HEADS])
        lru_hs.append(lru_h[:, 0, :])
        lru_bufs.append(lru_buf)
        r4 = r_out.reshape(b, npair, 2, HEAD_DIM, 2, HEAD_DIM)
        rets.append(jnp.stack([r4[:, :, 0, :, 0, :], r4[:, :, 1, :, 1, :]], axis=2).reshape(
            b, RET_HEADS, HEAD_DIM, HEAD_DIM))
    k_out = kv[0].reshape(depth, b, t, FOX_HEADS, HEAD_DIM)
    v_out = kv[1].reshape(depth, b, t, FOX_HEADS, HEAD_DIM)
    return x, (k_out, v_out, jnp.stack(logfs), jnp.stack(lru_hs), jnp.stack(lru_bufs), jnp.stack(rets))


def _pair_state(r):
    depth, b, h, d, _ = r.shape
    r6 = r.reshape(depth, b, h // 2, 2, d, d)
    eye = jnp.eye(2, dtype=r.dtype)
    out = r6[:, :, :, :, :, None, :] * eye[None, None, None, :, None, :, None]
    return out.reshape(depth, b, h // 2, 2 * d, 2 * d)


def kernel(x_prompt, x_sample, cache_k, cache_v, cache_logf, state_lru_h, state_lru_conv, state_ret, page_table,
           c_prompt, c_sample, w_in, w_out, conv_w, conv_b, lru_wa, lru_ba, lru_wx, lru_bx, lru_lambda, q_norm_w,
           k_norm_w, forget_bias, ret_norm_w, norm1_w, norm2_w, w_ada, b_ada, router_w, router_bias, w_gate, w_up,
           w_down):
    depth, d, _ = w_in.shape
    bp = x_prompt.shape[0]
    bs, ts, _ = x_sample.shape
    prm = _prep(w_in=w_in, w_out=w_out, conv_w=conv_w, conv_b=conv_b, lru_wa=lru_wa, lru_ba=lru_ba, lru_wx=lru_wx,
                lru_bx=lru_bx, lru_lambda=lru_lambda, q_norm_w=q_norm_w, k_norm_w=k_norm_w, forget_bias=forget_bias,
                ret_norm_w=ret_norm_w, norm1_w=norm1_w, norm2_w=norm2_w, router_w=router_w, router_bias=router_bias,
                w_gate=w_gate, w_up=w_up, w_down=w_down)

    mod = _ada(jnp.concatenate([c_prompt, c_sample], axis=0), w_ada, b_ada)
    mod_p = mod[:, :bp].reshape(depth, bp, 1, 6 * d)
    mod_s = jnp.repeat(mod[:, bp:], ts, axis=1).reshape(depth, 1, bs * ts, 6 * d)
    return _run(prm, mod_p, mod_s, x_prompt, x_sample, cache_k, cache_v, cache_logf, state_lru_h, state_lru_conv,
                state_ret, page_table)


def _prep(w_in, w_out, conv_w, conv_b, lru_wa, lru_ba, lru_wx, lru_bx, lru_lambda, q_norm_w, k_norm_w, forget_bias,
          ret_norm_w, norm1_w, norm2_w, router_w, router_bias, w_gate, w_up, w_down, **unused):
    depth, d, n_in = w_in.shape
    ff0 = 2 * LRU_WIDTH + 3 * FOX_WIDTH
    w_in_p = jnp.concatenate(
        [w_in[:, :, :ff0], w_in[:, :, ff0 + FOX_HEADS:], w_in[:, :, ff0:ff0 + FOX_HEADS],
         jnp.zeros((depth, d, N_PROJ - n_in), w_in.dtype)], axis=-1)
    rw_hi = router_w.T.astype(BF16)
    rw_lo = (router_w.T - rw_hi.astype(F32)).astype(BF16)
    return dict(
        w_in_hl=_split2(w_in_p),
        norm1_w=norm1_w.reshape(depth, 1, d),
        norm2_w=norm2_w.reshape(depth, 1, d),
        q_norm_w=jnp.tile(q_norm_w, (1, FOX_HEADS)).reshape(depth, 1, FOX_WIDTH),
        k_norm_w=jnp.tile(k_norm_w, (1, FOX_HEADS)).reshape(depth, 1, FOX_WIDTH),
        forget_bias=jnp.pad(forget_bias, ((0, 0), (0, LANES - FOX_HEADS))).reshape(depth, 1, LANES),
        g512=_group_mean_matrix(FOX_WIDTH),
        g128=_group_mean_matrix(LANES),
        conv_w=conv_w,
        conv_b=conv_b.reshape(depth, 1, LRU_WIDTH),
        lru_wa_hl=_split2(_block_diag(lru_wa)),
        lru_wx_hl=_split2(_block_diag(lru_wx)),
        lru_ba=lru_ba.reshape(depth, 1, LRU_WIDTH),
        lru_bx=lru_bx.reshape(depth, 1, LRU_WIDTH),
        lru_lambda=lru_lambda.reshape(depth, 1, LRU_WIDTH),
        ret_norm_w=ret_norm_w.reshape(depth, 1, RET_WIDTH),
        w_out_hl=_split2(w_out),
        rw_hi=rw_hi, rw_lo=rw_lo,
        router_bias=router_bias.reshape(N_EXPERTS, 1),
        w_gate16=w_gate.astype(BF16), w_up16=w_up.astype(BF16), w_down16=w_down.astype(BF16),
    )


def _run(prm, mod_p, mod_s, x_prompt, x_sample, cache_k, cache_v, cache_logf, state_lru_h, state_lru_conv, state_ret,
         page_table):
    depth = prm['w_in_hl'][0].shape[0]
    bp, tp, d = x_prompt.shape
    bs, ts, _ = x_sample.shape
    n_pages = page_table.shape[1]
    past = n_pages * cache_k.shape[2]

    npair = RET_HEADS // 2
    zeros = lambda *s: jnp.zeros(s, F32)
    tq = min(tp, 256)
    y_prompt, outs_p = _trunk(
        x_prompt, mod_p, False, (bp, tp), np.arange(tp),
        zeros(depth, bp, 1, LRU_WIDTH), zeros(depth, bp, CONV_W - 1, LRU_WIDTH),
        zeros(depth, bp, npair, LANES, LANES), min(tp, 128),
        lambda l, q, k, v, cum, cumt: _fox_prompt(q, k, v, cum, cumt, tq, F32),
        prm, min(tp, 512), min(tp, 1024), False)

    pp = 8 if n_pages % 8 == 0 else 1

    def attend_sample(l, q, k, v, cum, cumt):
        bias = _suffix_bias(l, cache_logf, page_table)
        return _fox_decode(l, q, k, v, cum, cache_k, cache_v, bias, page_table, pp, F32)

    n_s = bs * ts
    y_s, outs_s = _trunk(
        x_sample.reshape(1, n_s, d), mod_s, True, (bs, ts), past + np.tile(np.arange(ts), bs),
        state_lru_h.reshape(depth, bs, 1, LRU_WIDTH), state_lru_conv, _pair_state(state_ret), ts,
        attend_sample, prm, n_s, n_s, True)
    return (y_prompt, y_s.reshape(bs, ts, d)) + tuple(outs_p) + tuple(outs_s)
```

```python
import functools
import math

import jax
import jax.numpy as jnp
import numpy as np
from jax import lax
from jax.experimental import pallas as pl
from jax.experimental.pallas import tpu as pltpu

F32 = jnp.float32
BF16 = jnp.bfloat16

HEAD_DIM = 64
LRU_WIDTH = 256
LRU_BLOCKS = 4
CONV_W = 4
LRU_C = 8.0
FOX_WIDTH = 512
FOX_HEADS = 8
RET_WIDTH = 256
RET_HEADS = 4
ROPE_BASE = 10000.0
N_EXPERTS = 16
N_GROUPS = 4
EXPERTS_PER_GROUP = 4
EPS = 1e-6
LANES = 128
LOG2E = math.log2(math.e)
N_PROJ = 3200

VMEM_LIMIT = 56 * 1024 * 1024


def _cp(sem):
    return pltpu.CompilerParams(dimension_semantics=sem, vmem_limit_bytes=VMEM_LIMIT)


def _dot(a, b):
    return jnp.dot(a, b, preferred_element_type=F32)


def _dot_nt(a, b):
    return lax.dot_general(a, b, (((1,), (1,)), ((), ())), preferred_element_type=F32)


def _split2(x):
    hi = x.astype(BF16)
    lo = (x - hi.astype(F32)).astype(BF16)
    return hi, lo


def _split3(x):
    hi = x.astype(BF16)
    r = x - hi.astype(F32)
    mid = r.astype(BF16)
    lo = (r - mid.astype(F32)).astype(BF16)
    return hi, mid, lo


def _dot_f32(a, b):
    ah, al = _split2(a)
    bh, bl = _split2(b)
    return _dot(ah, bh) + (_dot(ah, bl) + _dot(al, bh))


def _dot_exact_rhs(a, m):
    hi, mid, lo = _split3(a)
    return _dot(hi, m) + (_dot(mid, m) + _dot(lo, m))


def _lhs(a, precise):
    return _split2(a.astype(F32)) if precise else (a.astype(BF16),)


def _mm(lhs, w_refs, cols=None):
    sl = (lambda r: r[...]) if cols is None else (lambda r: r[cols[0], cols[1]])
    if len(w_refs) == 1:
        return _dot(lhs[0], sl(w_refs[0]))
    wh, wl = sl(w_refs[0]), sl(w_refs[1])
    return _dot(lhs[0], wh) + (_dot(lhs[0], wl) + _dot(lhs[1], wh))


def _dx(a, b, precise, nt=False):
    f = _dot_nt if nt else _dot
    if not precise:
        return f(a.astype(BF16), b.astype(BF16))
    ah, al = _split2(a.astype(F32))
    bh, bl = _split2(b.astype(F32))
    return f(ah, bh) + (f(ah, bl) + f(al, bh))


def _idiv(x, n):
    assert n & (n - 1) == 0
    return lax.shift_right_logical(x, n.bit_length() - 1)


def _sigmoid(x):
    return 1.0 / (1.0 + jnp.exp(-x))


def _silu(x):
    return x * _sigmoid(x)


def _softplus(x):
    return jnp.maximum(x, 0.0) + jnp.log1p(jnp.exp(-jnp.abs(x)))


def _gelu_tanh(x):
    c = math.sqrt(2.0 / math.pi)
    return 0.5 * x * (1.0 + jnp.tanh(c * (x + 0.044715 * (x * x * x))))


def _rms(x, w):
    ms = jnp.mean(x * x, axis=-1, keepdims=True)
    return x * lax.rsqrt(ms + EPS) * w


def _ada_kernel(c_ref, w_ref, b_ref, o_ref):
    s = _silu(c_ref[...])
    o_ref[...] = _dot_f32(s, w_ref[...]) + b_ref[...]


def _ada(c_all, w_ada, b_ada):
    depth, d, n6 = w_ada.shape
    bc = c_all.shape[0]
    tn = 1024
    return pl.pallas_call(
        _ada_kernel,
        out_shape=jax.ShapeDtypeStruct((depth, bc, n6), F32),
        grid=(depth, n6 // tn),
        in_specs=[
            pl.BlockSpec((bc, d), lambda l, j: (0, 0)),
            pl.BlockSpec((None, d, tn), lambda l, j: (l, 0, j)),
            pl.BlockSpec((None, 1, tn), lambda l, j: (l, 0, j)),
        ],
        out_specs=pl.BlockSpec((None, bc, tn), lambda l, j: (l, 0, j)),
        compiler_params=_cp(("parallel", "parallel")),
        name="ada_mod",
    )(c_all, w_ada, b_ada.reshape(depth, 1, n6))


def _proj_kernel(has_alias, precise, q_scale, x_ref, sh_ref, sc_ref, n1_ref, qw_ref, kw_ref, fb_ref, g_ref,
                 cos_ref, sin_ref, *rest):
    nw = 2 if precise else 1
    w_refs, rest = rest[:nw], rest[nw:]
    if has_alias:
        rest = rest[2:]
    xbgb_ref, q_ref, k32_ref, k16_ref, v32_ref, v16_ref, rqkv_ref, rg_ref, lf_ref = rest
    x = x_ref[...]
    h = _rms(x, n1_ref[...]) * (1.0 + sc_ref[...]) + sh_ref[...]
    hl = _lhs(h, precise)
    g = g_ref[...]

    def seg(lo, hi):
        return _mm(hl, w_refs, (slice(None), slice(lo, hi)))

    xbgb_ref[...] = seg(0, 512)

    def head_norm(t, w):
        hi, lo = _split2(t * t)
        ms = _dot(hi, g) + _dot(lo, g)
        return t * lax.rsqrt(ms + EPS) * w

    q = head_norm(seg(512, 1024), qw_ref[...])
    q_ref[...] = (q * q_scale).astype(q_ref.dtype)

    def store_layer(ref, val):
        if has_alias:
            ref[...] = val
        else:
            ref[0] = val
            ref[1:] = jnp.zeros((ref.shape[0] - 1,) + val.shape, val.dtype)

    k = head_norm(seg(1024, 1536), kw_ref[...])
    store_layer(k32_ref, k)
    k16_ref[...] = k.astype(BF16)
    v = seg(1536, 2048)
    store_layer(v32_ref, v)
    v16_ref[...] = v.astype(BF16)

    cos = cos_ref[...]
    sin = sin_ref[...]
    first_half = (lax.broadcasted_iota(jnp.int32, cos.shape, 1) & 32) == 0

    def rope(t):
        rot = jnp.where(first_half, -pltpu.roll(t, RET_WIDTH - 32, 1), pltpu.roll(t, 32, 1))
        return t * cos + rot * sin

    rdt = rqkv_ref.dtype
    rqkv_ref[:, 0:256] = rope(seg(2048, 2304)).astype(rdt)
    rqkv_ref[:, 256:512] = (rope(seg(2304, 2560)) * (HEAD_DIM ** -0.5)).astype(rdt)
    rqkv_ref[:, 512:768] = seg(2560, 2816).astype(rdt)
    rg_ref[...] = seg(2816, 3072)

    ff = seg(3072, 3200) + fb_ref[...]
    lf_ref[...] = -_softplus(-ff)


def _proj(layer, x, mod, per_token, n1, w_in_hl, qw, kw, fb, gmat, cos_t, sin_t, kv_prev, tm, precise, adt, q_scale):
    b, t, d = x.shape
    depth = w_in_hl[0].shape[0]
    nt = t // tm
    tmm = tm if per_token else 1
    mi = (lambda bi, ti: ti) if per_token else (lambda bi, ti: 0)
    has_alias = kv_prev is not None
    w_list = list(w_in_hl) if precise else [w_in_hl[0]]
    in_specs = [
        pl.BlockSpec((None, tm, d), lambda bi, ti: (bi, ti, 0)),
        pl.BlockSpec((None, tmm, d), lambda bi, ti: (bi, mi(bi, ti), 0)),
        pl.BlockSpec((None, tmm, d), lambda bi, ti: (bi, mi(bi, ti), 1)),
        pl.BlockSpec((None, 1, d), lambda bi, ti: (layer, 0, 0)),
        pl.BlockSpec((None, 1, FOX_WIDTH), lambda bi, ti: (layer, 0, 0)),
        pl.BlockSpec((None, 1, FOX_WIDTH), lambda bi, ti: (layer, 0, 0)),
        pl.BlockSpec((None, 1, LANES), lambda bi, ti: (layer, 0, 0)),
        pl.BlockSpec((FOX_WIDTH, FOX_WIDTH), lambda bi, ti: (0, 0)),
        pl.BlockSpec((tm, RET_WIDTH), lambda bi, ti: (ti, 0)),
        pl.BlockSpec((tm, RET_WIDTH), lambda bi, ti: (ti, 0)),
    ] + [pl.BlockSpec((None, d, N_PROJ), lambda bi, ti: (layer, 0, 0), pipeline_mode=pl.Buffered(1)) for _ in w_list]
    args = [x, mod, mod, n1, qw, kw, fb, gmat, cos_t, sin_t] + w_list
    aliases = {}
    if has_alias:
        in_specs += [pl.BlockSpec(memory_space=pl.ANY), pl.BlockSpec(memory_space=pl.ANY)]
        aliases = {len(args): 2, len(args) + 1: 4}
        args += list(kv_prev)
    tok = lambda w, dt: jax.ShapeDtypeStruct((b, t, w), dt)
    kv_shape = jax.ShapeDtypeStruct((depth, b, t, FOX_WIDTH), F32)
    tok_spec = lambda w: pl.BlockSpec((None, tm, w), lambda bi, ti: (bi, ti, 0))
    if has_alias:
        kv_spec = pl.BlockSpec((None, None, tm, FOX_WIDTH), lambda bi, ti: (layer, bi, ti, 0))
    else:
        assert layer == 0
        kv_spec = pl.BlockSpec((depth, None, tm, FOX_WIDTH), lambda bi, ti: (0, bi, ti, 0))
    return pl.pallas_call(
        functools.partial(_proj_kernel, has_alias, precise, q_scale),
        out_shape=(tok(512, F32), tok(FOX_WIDTH, adt), kv_shape, tok(FOX_WIDTH, BF16), kv_shape,
                   tok(FOX_WIDTH, BF16), tok(768, adt), tok(RET_WIDTH, F32), tok(LANES, F32)),
        grid=(b, nt),
        in_specs=in_specs,
        out_specs=(tok_spec(512), tok_spec(FOX_WIDTH), kv_spec, tok_spec(FOX_WIDTH), kv_spec,
                   tok_spec(FOX_WIDTH), tok_spec(768), tok_spec(RET_WIDTH), tok_spec(LANES)),
        input_output_aliases=aliases,
        compiler_params=_cp(("parallel", "parallel")),
        name="norm_proj",
    )(*args)


def _lru_kernel(ch, precise, x_ref, h0_ref, buf_ref, cw_ref, cb_ref, ba_ref, bx_ref, lam_ref, *rest):
    nw = 2 if precise else 1
    wa_refs, wx_refs, rest = rest[:nw], rest[nw:2 * nw], rest[2 * nw:]
    y_ref, hl_ref, bo_ref, xpad, a_s, b_s, h_s, hc = rest
    ci = pl.program_id(1)

    @pl.when(ci == 0)
    def _():
        xpad[0:8, :] = jnp.zeros((8, LRU_WIDTH), F32)
        xpad[5:8, :] = buf_ref[...]
        hc[...] = jnp.broadcast_to(h0_ref[...], (8, LRU_WIDTH))

    xb = x_ref[:, 0:LRU_WIDTH]
    gb = x_ref[:, LRU_WIDTH:2 * LRU_WIDTH]
    xpad[8:8 + ch, :] = xb
    xc = cw_ref[0:1, :] * xpad[5:5 + ch, :]
    xc = xc + cw_ref[1:2, :] * xpad[6:6 + ch, :]
    xc = xc + cw_ref[2:3, :] * xpad[7:7 + ch, :]
    xc = xc + cw_ref[3:4, :] * xb
    xc = xc + cb_ref[...]
    xl = _lhs(xc, precise)
    r = _sigmoid(_mm(xl, wa_refs) + ba_ref[...])
    i = _sigmoid(_mm(xl, wx_refs) + bx_ref[...])
    log_a = (-LRU_C) * r * _softplus(-lam_ref[...])
    a_s[...] = jnp.exp(log_a)
    th = jnp.tanh(log_a)
    b_s[...] = jnp.sqrt(-2.0 * th / (1.0 - th)) * i * xc

    row = lax.broadcasted_iota(jnp.int32, (8, LRU_WIDTH), 0)

    def blk(j, h):
        r0 = pl.multiple_of(j * 8, 8)
        a = a_s[pl.ds(r0, 8), :]
        bv = b_s[pl.ds(r0, 8), :]
        for s in (1, 2, 4):
            keep = row >= s
            a_sh = jnp.where(keep, pltpu.roll(a, s, 0), 1.0)
            b_sh = jnp.where(keep, pltpu.roll(bv, s, 0), 0.0)
            bv = a * b_sh + bv
            a = a * a_sh
        hb = a * h + bv
        h_s[pl.ds(r0, 8), :] = hb
        return jnp.broadcast_to(hb[7:8, :], (8, LRU_WIDTH))

    h_end = lax.fori_loop(0, ch // 8, blk, hc[...])
    hc[...] = h_end
    y_ref[...] = (h_s[...] * _gelu_tanh(gb)).astype(y_ref.dtype)
    tail = xpad[ch:ch + 8, :]
    xpad[0:8, :] = tail

    @pl.when(ci == pl.num_programs(1) - 1)
    def _():
        hl_ref[...] = h_end[0:1, :]
        bo_ref[...] = tail[5:8, :]


def _lru(xbgb, h0, buf, cw, cb, wa_hl, ba, wx_hl, bx, lam, precise, ydt):
    b, t, _ = xbgb.shape
    ch = min(t, 256)
    w = LRU_WIDTH
    vec = lambda: pl.BlockSpec((1, w), lambda bi, ci: (0, 0))
    mats = (list(wa_hl) + list(wx_hl)) if precise else [wa_hl[0], wx_hl[0]]
    return pl.pallas_call(
        functools.partial(_lru_kernel, ch, precise),
        out_shape=(jax.ShapeDtypeStruct((b, t, w), ydt), jax.ShapeDtypeStruct((b, 1, w), F32),
                   jax.ShapeDtypeStruct((b, CONV_W - 1, w), F32)),
        grid=(b, t // ch),
        in_specs=[
            pl.BlockSpec((None, ch, 2 * w), lambda bi, ci: (bi, ci, 0)),
            pl.BlockSpec((None, 1, w), lambda bi, ci: (bi, 0, 0)),
            pl.BlockSpec((None, CONV_W - 1, w), lambda bi, ci: (bi, 0, 0)),
            pl.BlockSpec((CONV_W, w), lambda bi, ci: (0, 0)),
            vec(), vec(), vec(), vec(),
        ] + [pl.BlockSpec((w, w), lambda bi, ci: (0, 0)) for _ in mats],
        out_specs=(pl.BlockSpec((None, ch, w), lambda bi, ci: (bi, ci, 0)),
                   pl.BlockSpec((None, 1, w), lambda bi, ci: (bi, 0, 0)),
                   pl.BlockSpec((None, CONV_W - 1, w), lambda bi, ci: (bi, 0, 0))),
        scratch_shapes=[pltpu.VMEM((ch + 8, w), F32), pltpu.VMEM((ch, w), F32), pltpu.VMEM((ch, w), F32),
                        pltpu.VMEM((ch, w), F32), pltpu.VMEM((8, w), F32)],
        compiler_params=_cp(("parallel", "arbitrary")),
        name="rglru",
    )(xbgb, h0, buf, cw, cb, ba, bx, lam, *mats)


def _cum_kernel(t, tc, tscale, lf_ref, cum_ref, cumt_ref, carry):
    ci = pl.program_id(1)

    @pl.when(ci == 0)
    def _():
        carry[...] = jnp.zeros_like(carry)

    lf = lf_ref[...]
    if t < tc:
        lf = jnp.concatenate([lf, jnp.zeros((tc - t, LANES), F32)], axis=0)
    r = lax.broadcasted_iota(jnp.int32, (tc, tc), 0)
    c = lax.broadcasted_iota(jnp.int32, (tc, tc), 1)
    tri = jnp.where(c <= r, 1.0, 0.0).astype(BF16)
    hi, mid, lo = _split3(lf)
    cum = _dot(tri, hi) + (_dot(tri, mid) + _dot(tri, lo)) + carry[0:1, :]
    carry[...] = jnp.broadcast_to(cum[tc - 1:tc, :], carry.shape)
    cum_ref[...] = cum[0:t, :] if t < tc else cum
    cumt_ref[...] = cum.T[0:FOX_HEADS, :] * tscale


def _cum(lf, tscale):
    b, t, _ = lf.shape
    tc = 128 if t < 128 else min(t, 512)
    tb = min(t, tc)
    nt = max(t // tc, 1)
    return pl.pallas_call(
        functools.partial(_cum_kernel, tb, tc, tscale),
        out_shape=(jax.ShapeDtypeStruct((b, t, LANES), F32), jax.ShapeDtypeStruct((b, FOX_HEADS, nt * tc), F32)),
        grid=(b, nt),
        in_specs=[pl.BlockSpec((None, tb, LANES), lambda bi, ci: (bi, ci, 0))],
        out_specs=(pl.BlockSpec((None, tb, LANES), lambda bi, ci: (bi, ci, 0)),
                   pl.BlockSpec((None, FOX_HEADS, tc), lambda bi, ci: (bi, 0, ci))),
        scratch_shapes=[pltpu.VMEM((8, LANES), F32)],
        compiler_params=_cp(("parallel", "arbitrary")),
        name="logf_cumsum",
    )(lf)


def _fox_kernel(tq, ts, q_ref, k_ref, v_ref, cumt_ref, o_ref, m_s, l_s, acc_s):
    hp = pl.program_id(1)
    qi = pl.program_id(2)
    n_strips = tq // ts
    q = q_ref[...]
    lane = lax.broadcasted_iota(jnp.int32, (tq, LANES), 1)
    low = lane < HEAD_DIM
    zero = jnp.zeros_like(q)
    qm = (jnp.where(low, q, zero), jnp.where(low, zero, q))
    m_s[...] = jnp.full_like(m_s, -jnp.inf)
    l_s[...] = jnp.zeros_like(l_s)
    acc_s[...] = jnp.zeros_like(acc_s)

    def strip(hh, r, k, v, ck, diagonal, state):
        ncol = (r + 1) * ts if diagonal else tq
        s = _dot_nt(qm[hh][r * ts:(r + 1) * ts, :], k[0:ncol, :]) - ck[:, 0:ncol]
        if diagonal:
            rr = lax.broadcasted_iota(jnp.int32, (ts, ncol), 0)
            cc = lax.broadcasted_iota(jnp.int32, (ts, ncol), 1)
            s = jnp.where(cc <= rr + r * ts, s, -jnp.inf)
        m_prev, l_prev, acc_prev = state
        m_new = jnp.maximum(m_prev, jnp.max(s, axis=-1, keepdims=True))
        alpha = jnp.exp2(m_prev - m_new)
        p = jnp.exp2(s - jnp.concatenate([m_new] * (ncol // LANES), axis=1))
        l_new = alpha * l_prev + jnp.sum(p, axis=-1, keepdims=True)
        acc_new = alpha * acc_prev + _dot(p.astype(BF16), v[0:ncol, :])
        return m_new, l_new, acc_new

    def tile(ki, diagonal):
        k0 = pl.multiple_of(ki * tq, tq)
        k = k_ref[pl.ds(k0, tq), :]
        v = v_ref[pl.ds(k0, tq), :]
        chains = [(hh, r) for hh in range(2) for r in range(n_strips)]
        rows = lambda r: slice(r * ts, (r + 1) * ts)
        states = [(m_s[hh, rows(r), :], l_s[hh, rows(r), :], acc_s[hh, rows(r), :]) for hh, r in chains]
        cks = [cumt_ref[pl.ds(2 * hp + hh, 1), pl.ds(k0, tq)] for hh in range(2)]
        new = [strip(hh, r, k, v, cks[hh], diagonal, st) for (hh, r), st in zip(chains, states)]
        for (hh, r), (m_new, l_new, acc_new) in zip(chains, new):
            m_s[hh, rows(r), :] = m_new
            l_s[hh, rows(r), :] = l_new
            acc_s[hh, rows(r), :] = acc_new

    def body(ki, c):
        tile(ki, False)
        return c

    lax.fori_loop(0, qi, body, 0)
    tile(qi, True)
    o_ref[...] = jnp.where(low, acc_s[0] / l_s[0], acc_s[1] / l_s[1]).astype(o_ref.dtype)


def _fox_prompt(q16, k16, v16, cumt2, tq, ydt):
    b, t, _ = q16.shape
    npair = FOX_HEADS // 2
    ts = min(tq, LANES)
    return pl.pallas_call(
        functools.partial(_fox_kernel, tq, ts),
        out_shape=jax.ShapeDtypeStruct((b, t, FOX_WIDTH), ydt),
        grid=(b, npair, t // tq),
        in_specs=[
            pl.BlockSpec((None, tq, LANES), lambda bi, hp, qi: (bi, qi, hp)),
            pl.BlockSpec((None, t, LANES), lambda bi, hp, qi: (bi, 0, hp)),
            pl.BlockSpec((None, t, LANES), lambda bi, hp, qi: (bi, 0, hp)),
            pl.BlockSpec((None, FOX_HEADS, t), lambda bi, hp, qi: (bi, 0, 0)),
        ],
        out_specs=pl.BlockSpec((None, tq, LANES), lambda bi, hp, qi: (bi, qi, hp)),
        scratch_shapes=[pltpu.VMEM((2, tq, LANES), F32), pltpu.VMEM((2, tq, LANES), F32),
                        pltpu.VMEM((2, tq, LANES), F32)],
        compiler_params=_cp(("parallel", "parallel", "arbitrary")),
        name="fox_prompt",
    )(q16, k16, v16, cumt2)


def _suffix_kernel(layer, n_pages, nh, ps, pt_ref, lf_hbm, later_ref, o_ref, xbuf, sem):
    bi = pl.program_id(0)

    def copy(p):
        return pltpu.make_async_copy(lf_hbm.at[layer, pt_ref[bi, p]], xbuf.at[p], sem)

    def start(p, c):
        copy(p).start()
        return c

    def wait(p, c):
        copy(p).wait()
        return c

    lax.fori_loop(0, n_pages, start, 0)
    lax.fori_loop(0, n_pages, wait, 0)
    x = xbuf[...].reshape(n_pages * nh, ps)
    r = lax.broadcasted_iota(jnp.int32, (ps, ps), 0)
    c = lax.broadcasted_iota(jnp.int32, (ps, ps), 1)
    after = jnp.where(r > c, 1.0, 0.0).astype(BF16)
    within = _dot_exact_rhs(x, after)
    tot = _dot_exact_rhs(x, jnp.ones((ps, ps), BF16))
    hi, mid, lo = _split3(tot)
    later = later_ref[...]
    carry = _dot(later, hi) + (_dot(later, mid) + _dot(later, lo))
    o_ref[...] = (within + carry).reshape(n_pages, nh, ps)


def _suffix_bias(layer, cache_logf, page_table):
    depth, n_pool, ps, nh = cache_logf.shape
    b, n_pages = page_table.shape
    lf = cache_logf.transpose(0, 1, 3, 2)
    idx = np.arange(n_pages * nh)
    pg, head = idx // nh, idx % nh
    later = jnp.asarray((head[:, None] == head[None, :]) & (pg[None, :] > pg[:, None]), BF16)
    grid_spec = pltpu.PrefetchScalarGridSpec(
        num_scalar_prefetch=1,
        grid=(b,),
        in_specs=[
            pl.BlockSpec(memory_space=pl.ANY),
            pl.BlockSpec((n_pages * nh, n_pages * nh), lambda bi, pt: (0, 0)),
        ],
        out_specs=pl.BlockSpec((None, n_pages, nh, ps), lambda bi, pt: (bi, 0, 0, 0)),
        scratch_shapes=[pltpu.VMEM((n_pages, nh, ps), F32), pltpu.SemaphoreType.DMA(())],
    )
    return pl.pallas_call(
        functools.partial(_suffix_kernel, layer, n_pages, nh, ps),
        out_shape=jax.ShapeDtypeStruct((b, n_pages, nh, ps), F32),
        grid_spec=grid_spec,
        compiler_params=_cp(("arbitrary",)),
        name="logf_suffix",
    )(page_table, lf, later)


def _decode_kernel(pp, nh, ps, s_new, pt_ref, q_ref, cb_ref, bias_ref, kn_ref, vn_ref, nb_ref, *rest):
    k_refs = rest[:pp]
    v_refs = rest[pp:2 * pp]
    o_ref, m_s, l_s, acc_s = rest[2 * pp:]
    j = pl.program_id(1)
    rows = nh * s_new
    w = nh * HEAD_DIM

    @pl.when(j == 0)
    def _():
        m_s[...] = jnp.full_like(m_s, -jnp.inf)
        l_s[...] = jnp.zeros_like(l_s)
        acc_s[...] = jnp.zeros_like(acc_s)

    q = q_ref[...]
    cb = cb_ref[...]

    def qk(kt):
        s2 = _dot(q, kt)
        return s2[0:rows, :] + s2[rows:2 * rows, :]

    def step(s_list, vt_list):
        m = m_s[...]
        m_new = m
        for s in s_list:
            m_new = jnp.maximum(m_new, jnp.max(s, axis=-1, keepdims=True))
        alpha = jnp.exp(m - m_new)
        l = alpha * l_s[...]
        acc = alpha * acc_s[...]
        for s, vt in zip(s_list, vt_list):
            p = jnp.exp(s - m_new)
            l = l + jnp.sum(p, axis=-1, keepdims=True)
            acc = acc + _dot_nt(p.astype(BF16), vt)
        m_s[...] = m_new
        l_s[...] = l
        acc_s[...] = acc

    s_list, vt_list = [], []
    for r in range(pp):
        kt = k_refs[r][...].reshape(w, ps).astype(BF16)
        bias = bias_ref[r]
        bias = jnp.concatenate([jnp.broadcast_to(bias[h:h + 1, :], (s_new, ps)) for h in range(nh)], axis=0)
        s_list.append(qk(kt) + (bias + cb))
        vt_list.append(v_refs[r][...].reshape(w, ps).astype(BF16))
    step(s_list, vt_list)

    @pl.when(j == pl.num_programs(1) - 1)
    def _():
        rr = lax.broadcasted_iota(jnp.int32, (rows, ps), 0)
        cc = lax.broadcasted_iota(jnp.int32, (rows, ps), 1)
        ok = cc <= (rr & (s_new - 1))
        s = qk(kn_ref[...]) + (cb - nb_ref[...])
        step([jnp.where(ok, s, -jnp.inf)], [vn_ref[...]])
        o = acc_s[...] / l_s[...]
        rh = _idiv(lax.broadcasted_iota(jnp.int32, (rows, w), 0), s_new)
        ch = _idiv(lax.broadcasted_iota(jnp.int32, (rows, w), 1), HEAD_DIM)
        o = jnp.where(rh == ch, o, 0.0)
        out = o[0:s_new, :]
        for h in range(1, nh):
            out = out + o[h * s_new:(h + 1) * s_new, :]
        o_ref[...] = out.astype(o_ref.dtype)


def _fox_decode(layer, q32, k16, v16, cum, cache_k, cache_v, bias, page_table, pp, ydt):
    depth, n_pool, ps, nh, hd = cache_k.shape
    b, s_new, w = q32.shape
    n_pages = page_table.shape[1]
    rows = nh * s_new
    assert s_new & (s_new - 1) == 0 and nh & (nh - 1) == 0 and s_new <= ps
    ckt = cache_k.transpose(0, 1, 3, 4, 2)
    cvt = cache_v.transpose(0, 1, 3, 4, 2)
    q4 = q32.reshape(b, s_new, nh, hd)
    eye = jnp.eye(nh, dtype=F32)
    qbd = (q4.transpose(0, 2, 1, 3)[:, :, :, None, :] * eye[None, :, None, :, None]).reshape(b, rows, w)
    qbd_hi = qbd.astype(BF16)
    qbd = jnp.concatenate([qbd_hi, (qbd - qbd_hi.astype(F32)).astype(BF16)], axis=1)
    cum_h = cum[:, :, :nh].transpose(0, 2, 1)
    colb = cum_h.reshape(b, rows, 1)
    newb = jnp.pad(jnp.repeat(cum_h, s_new, axis=1), ((0, 0), (0, 0), (0, ps - s_new)))
    padt = lambda a: jnp.pad(a.transpose(0, 2, 1), ((0, 0), (0, 0), (0, ps - s_new)))
    knt = padt(k16)
    vnt = padt(v16)

    def page_spec(r):
        return pl.BlockSpec((None, None, nh, hd, ps), lambda bi, j, pt: (layer, pt[bi, j * pp + r], 0, 0, 0))

    grid_spec = pltpu.PrefetchScalarGridSpec(
        num_scalar_prefetch=1,
        grid=(b, n_pages // pp),
        in_specs=[
            pl.BlockSpec((None, 2 * rows, w), lambda bi, j, pt: (bi, 0, 0)),
            pl.BlockSpec((None, rows, 1), lambda bi, j, pt: (bi, 0, 0)),
            pl.BlockSpec((None, pp, nh, ps), lambda bi, j, pt: (bi, j, 0, 0)),
            pl.BlockSpec((None, w, ps), lambda bi, j, pt: (bi, 0, 0)),
            pl.BlockSpec((None, w, ps), lambda bi, j, pt: (bi, 0, 0)),
            pl.BlockSpec((None, rows, ps), lambda bi, j, pt: (bi, 0, 0)),
        ] + [page_spec(r) for r in range(pp)] + [page_spec(r) for r in range(pp)],
        out_specs=pl.BlockSpec((None, s_new, w), lambda bi, j, pt: (bi, 0, 0)),
        scratch_shapes=[pltpu.VMEM((rows, 1), F32), pltpu.VMEM((rows, 1), F32), pltpu.VMEM((rows, w), F32)],
    )
    return pl.pallas_call(
        functools.partial(_decode_kernel, pp, nh, ps, s_new),
        out_shape=jax.ShapeDtypeStruct((b, s_new, w), ydt),
        grid_spec=grid_spec,
        compiler_params=_cp(("parallel", "arbitrary")),
        name="fox_decode",
    )(page_table, qbd, colb, bias, knt, vnt, newb, *([ckt] * pp), *([cvt] * pp))


def _ret_kernel(c, n_chunks, precise, q_ref, k_ref, v_ref, rg_ref, r0_ref, dm_ref, qd_ref, kd_ref, cd_ref, nw_ref, g_ref,
                y_ref, ro_ref, r_s):
    cp = max(c, LANES)
    lane = lax.broadcasted_iota(jnp.int32, (c, LANES), 1)
    low = lane < HEAD_DIM
    br = _idiv(lax.broadcasted_iota(jnp.int32, (LANES, LANES), 0), HEAD_DIM)
    bc = _idiv(lax.broadcasted_iota(jnp.int32, (LANES, LANES), 1), HEAD_DIM)
    blockdiag = br == bc
    g = g_ref[...]
    r_s[...] = r0_ref[...]

    def gmean(t):
        hi, lo = _split2(t)
        return _dot(hi, g) + _dot(lo, g)

    def chunk(ci, carry):
        r0 = ci * c if isinstance(ci, int) else pl.multiple_of(ci * c, c)
        q = q_ref[pl.ds(r0, c), :]
        k = k_ref[pl.ds(r0, c), :].astype(F32)
        v = v_ref[pl.ds(r0, c), :].astype(F32)
        if c < cp:
            pad = jnp.zeros((cp - c, LANES), F32)
            k = jnp.concatenate([k, pad], axis=0)
            v = jnp.concatenate([v, pad], axis=0)
        if not precise:
            v = v.astype(BF16)
        rmat = r_s[...]
        zero = jnp.zeros_like(q)
        o = _dx(q, rmat, precise) * qd_ref[...]
        for hh in range(2):
            sel = low if hh == 0 else jnp.logical_not(low)
            s = _dx(jnp.where(sel, q, zero), k, precise, nt=True) * dm_ref[hh]
            oh = _dx(s, v, precise)
            o = o + jnp.where(sel, oh, 0.0)
        rn = _dx((k * kd_ref[...]).T, v, precise)
        r_s[...] = rmat * cd_ref[...] + jnp.where(blockdiag, rn, 0.0)
        mu = gmean(o)
        oc = o - mu
        var = gmean(oc * oc)
        on = oc * lax.rsqrt(var + EPS) * nw_ref[...]
        y_ref[pl.ds(r0, c), :] = (on * _silu(rg_ref[pl.ds(r0, c), :])).astype(y_ref.dtype)
        return carry

    if n_chunks == 1:
        chunk(0, 0)
    else:
        lax.fori_loop(0, n_chunks, chunk, 0)
    ro_ref[...] = r_s[...]


def _ret_tables(c):
    cp = max(c, LANES)
    hidx = np.arange(RET_HEADS, dtype=np.float64)
    log_g = np.log1p(-np.exp2(-5.0 - hidx))
    j = np.arange(c, dtype=np.float64)
    diff = j[:, None] - j[None, :]
    dmask = np.where(diff >= 0, np.exp(log_g[:, None, None] * np.maximum(diff, 0.0)), 0.0)
    dm = np.zeros((RET_HEADS, c, cp))
    dm[:, :, :c] = dmask
    q_dec = np.exp(log_g[:, None] * (j + 1.0))
    k_dec = np.exp(log_g[:, None] * (c - 1.0 - j))
    c_dec = np.exp(log_g * c)
    npair = RET_HEADS // 2
    qd = np.zeros((npair, c, LANES))
    kd = np.zeros((npair, cp, LANES))
    cd = np.zeros((npair, 1, LANES))
    for p in range(npair):
        for hh in range(2):
            sl = slice(hh * HEAD_DIM, (hh + 1) * HEAD_DIM)
            qd[p, :, sl] = q_dec[2 * p + hh][:, None]
            kd[p, :c, sl] = k_dec[2 * p + hh][:, None]
            cd[p, 0, sl] = c_dec[2 * p + hh]
    f = lambda a: jnp.asarray(a, F32)
    return f(dm.reshape(npair, 2, c, cp)), f(qd), f(kd), f(cd)


def _retention(rqkv, rg, r0, nw, gpair, c, precise, ydt):
    b, t, _ = rg.shape
    npair = RET_HEADS // 2
    cp = max(c, LANES)
    dm, qd, kd, cd = _ret_tables(c)
    col = lambda off: pl.BlockSpec((None, t, LANES), lambda bi, p: (bi, 0, off + p))
    return pl.pallas_call(
        functools.partial(_ret_kernel, c, t // c, precise),
        out_shape=(jax.ShapeDtypeStruct((b, t, RET_WIDTH), ydt),
                   jax.ShapeDtypeStruct((b, npair, LANES, LANES), F32)),
        grid=(b, npair),
        in_specs=[
            col(0), col(npair), col(2 * npair),
            pl.BlockSpec((None, t, LANES), lambda bi, p: (bi, 0, p)),
            pl.BlockSpec((None, None, LANES, LANES), lambda bi, p: (bi, p, 0, 0)),
            pl.BlockSpec((None, 2, c, cp), lambda bi, p: (p, 0, 0, 0)),
            pl.BlockSpec((None, c, LANES), lambda bi, p: (p, 0, 0)),
            pl.BlockSpec((None, cp, LANES), lambda bi, p: (p, 0, 0)),
            pl.BlockSpec((None, 1, LANES), lambda bi, p: (p, 0, 0)),
            pl.BlockSpec((1, LANES), lambda bi, p: (0, p)),
            pl.BlockSpec((LANES, LANES), lambda bi, p: (0, 0)),
        ],
        out_specs=(pl.BlockSpec((None, t, LANES), lambda bi, p: (bi, 0, p)),
                   pl.BlockSpec((None, None, LANES, LANES), lambda bi, p: (bi, p, 0, 0))),
        scratch_shapes=[pltpu.VMEM((LANES, LANES), F32)],
        compiler_params=_cp(("parallel", "parallel")),
        name="retention",
    )(rqkv, rqkv, rqkv, rg, r0, dm, qd, kd, cd, nw, gpair)


def _route_rows(logits, bias):
    rows = [logits[e:e + 1, :] for e in range(N_EXPERTS)]
    m = rows[0]
    for r in rows[1:]:
        m = jnp.maximum(m, r)
    ex = [jnp.exp(r - m) for r in rows]
    z = ex[0]
    for e in ex[1:]:
        z = z + e
    score = [e / z for e in ex]
    sel = [score[e] + bias[e:e + 1, :] for e in range(N_EXPERTS)]
    top2, gscore = [], []
    for g in range(N_GROUPS):
        v = sel[g * EXPERTS_PER_GROUP:(g + 1) * EXPERTS_PER_GROUP]
        gs = None
        for i in range(EXPERTS_PER_GROUP):
            rank = None
            for j in range(EXPERTS_PER_GROUP):
                if j == i:
                    continue
                ahead = (v[j] >= v[i]) if j < i else (v[j] > v[i])
                ahead = jnp.where(ahead, 1.0, 0.0)
                rank = ahead if rank is None else rank + ahead
            keep = rank < 1.5
            top2.append(keep)
            term = jnp.where(keep, v[i], 0.0)
            gs = term if gs is None else gs + term
        gscore.append(gs)
    out = []
    for g in range(N_GROUPS):
        win = None
        for j in range(N_GROUPS):
            if j == g:
                continue
            c = (gscore[g] > gscore[j]) if j < g else (gscore[g] >= gscore[j])
            win = c if win is None else (win & c)
        for i in range(EXPERTS_PER_GROUP):
            e = g * EXPERTS_PER_GROUP + i
            out.append(jnp.where(win & top2[e], score[e], 0.0))
    tot = out[0]
    for o in out[1:]:
        tot = tot + o
    return jnp.concatenate([o / tot for o in out], axis=0)


def _out_kernel(precise, ya_ref, yb_ref, yc_ref, x_ref, g1_ref, sh_ref, sc_ref, n2_ref, rwh_ref, rwl_ref, rb_ref,
                *rest):
    nw = 2 if precise else 1
    w_refs, (x1_ref, h2_ref, cmb_ref) = rest[:nw], rest[nw:]
    mix = _mm(_lhs(ya_ref[...], precise), w_refs, (slice(0, 256), slice(None)))
    mix = mix + _mm(_lhs(yb_ref[...], precise), w_refs, (slice(256, 768), slice(None)))
    mix = mix + _mm(_lhs(yc_ref[...], precise), w_refs, (slice(768, 1024), slice(None)))
    x1 = x_ref[...] + g1_ref[...] * mix
    x1_ref[...] = x1
    h2 = _rms(x1, n2_ref[...]) * (1.0 + sc_ref[...]) + sh_ref[...]
    hi, lo = _split2(h2)
    h2_ref[...] = hi
    rwh = rwh_ref[...]
    logits = _dot_nt(rwh, hi) + (_dot_nt(rwh, lo) + _dot_nt(rwl_ref[...], hi))
    cmb_ref[...] = _route_rows(logits, rb_ref[...])


def _out_proj(layer, ya, yb, yc, x, mod, per_token, w_out_hl, n2, rwh, rwl, rb, tm, precise):
    b, t, d = x.shape
    tmm = tm if per_token else 1
    mi = (lambda bi, ti: ti) if per_token else (lambda bi, ti: 0)
    tok = lambda w: pl.BlockSpec((None, tm, w), lambda bi, ti: (bi, ti, 0))
    modspec = lambda col: pl.BlockSpec((None, tmm, d), lambda bi, ti: (bi, mi(bi, ti), col))
    w_list = list(w_out_hl) if precise else [w_out_hl[0]]
    return pl.pallas_call(
        functools.partial(_out_kernel, precise),
        out_shape=(jax.ShapeDtypeStruct((b, t, d), F32), jax.ShapeDtypeStruct((b, t, d), BF16),
                   jax.ShapeDtypeStruct((b, N_EXPERTS, t), F32)),
        grid=(b, t // tm),
        in_specs=[
            tok(LRU_WIDTH), tok(FOX_WIDTH), tok(RET_WIDTH), tok(d),
            modspec(2), modspec(3), modspec(4),
            pl.BlockSpec((None, 1, d), lambda bi, ti: (layer, 0, 0)),
            pl.BlockSpec((N_EXPERTS, d), lambda bi, ti: (0, 0)),
            pl.BlockSpec((N_EXPERTS, d), lambda bi, ti: (0, 0)),
            pl.BlockSpec((N_EXPERTS, 1), lambda bi, ti: (0, 0)),
        ] + [pl.BlockSpec((None, d, d), lambda bi, ti: (layer, 0, 0)) for _ in w_list],
        out_specs=(tok(d), tok(d), pl.BlockSpec((None, N_EXPERTS, tm), lambda bi, ti: (bi, 0, ti))),
        compiler_params=_cp(("parallel", "parallel")),
        name="out_proj_route",
    )(ya, yb, yc, x, mod, mod, mod, n2, rwh, rwl, rb, *w_list)


def _moe_kernel(h_ref, c_ref, x1_ref, g2_ref, wg_ref, wu_ref, wd_ref, o_ref, acc):
    e = pl.program_id(2)

    @pl.when(e == 0)
    def _():
        acc[...] = jnp.zeros_like(acc)

    h = h_ref[...]
    act = _silu(_dot(h, wg_ref[...])) * _dot(h, wu_ref[...])
    cmb = c_ref[...]
    lane = lax.broadcasted_iota(jnp.int32, cmb.shape, 1)
    ce = jnp.sum(jnp.where(lane == e, cmb, 0.0), axis=-1, keepdims=True)
    acc[...] += _dot((ce * act).astype(BF16), wd_ref[...])

    @pl.when(e == pl.num_programs(2) - 1)
    def _():
        o_ref[...] = x1_ref[...] + g2_ref[...] * acc[...]


def _moe(layer, h2, cmb, x1, mod, per_token, wg, wu, wd, tm):
    b, t, d = x1.shape
    de = wg.shape[-1]
    tmm = tm if per_token else 1
    mi = (lambda bi, ti: ti) if per_token else (lambda bi, ti: 0)
    tok = lambda w: pl.BlockSpec((None, tm, w), lambda bi, ti, e: (bi, ti, 0))
    return pl.pallas_call(
        _moe_kernel,
        out_shape=jax.ShapeDtypeStruct((b, t, d), F32),
        grid=(b, t // tm, N_EXPERTS),
        in_specs=[
            tok(d), tok(N_EXPERTS), tok(d),
            pl.BlockSpec((None, tmm, d), lambda bi, ti, e: (bi, mi(bi, ti), 5)),
            pl.BlockSpec((None, None, d, de), lambda bi, ti, e: (layer, e, 0, 0)),
            pl.BlockSpec((None, None, d, de), lambda bi, ti, e: (layer, e, 0, 0)),
            pl.BlockSpec((None, None, de, d), lambda bi, ti, e: (layer, e, 0, 0)),
        ],
        out_specs=tok(d),
        scratch_shapes=[pltpu.VMEM((tm, d), F32)],
        compiler_params=_cp(("parallel", "parallel", "arbitrary")),
        name="moe_dense",
    )(h2, cmb, x1, mod, wg, wu, wd)


def _rope_tables(pos):
    half = HEAD_DIM // 2
    inv = np.exp(-np.log(ROPE_BASE) * np.arange(half, dtype=np.float64) / half)
    ang = np.asarray(pos, np.float64)[:, None] * inv[None, :]
    reps = RET_WIDTH // half
    return (jnp.asarray(np.tile(np.cos(ang), (1, reps)), F32), jnp.asarray(np.tile(np.sin(ang), (1, reps)), F32))


def _block_diag(w):
    depth, nb, h, _ = w.shape
    eye = jnp.eye(nb, dtype=w.dtype)
    return (w[:, :, :, None, :] * eye[None, :, None, :, None]).reshape(depth, nb * h, nb * h)


def _group_mean_matrix(width):
    idx = np.arange(width) // HEAD_DIM
    return jnp.asarray((idx[:, None] == idx[None, :]) / HEAD_DIM, BF16)


def _trunk(x, mod, per_token, seq_shape, pos, lru_h0, lru_buf0, ret_r0, ret_chunk, attend, prm, tm, tm_moe, precise,
           log2_scores):
    bv, tv, d = x.shape
    b, t = seq_shape
    depth = prm['w_in_hl'][0].shape[0]
    cos_t, sin_t = _rope_tables(pos)
    kv = None
    logfs, lru_hs, lru_bufs, rets = [], [], [], []
    npair = RET_HEADS // 2
    for l in range(depth):
        m = mod[l]
        xbgb, q16, k32, k16, v32, v16, rqkv, rg, lf = _proj(
            l, x, m, per_token, prm['norm1_w'], prm['w_in_hl'], prm['q_norm_w'], prm['k_norm_w'], prm['forget_bias'],
            prm['g512'], cos_t, sin_t, kv, tm, precise or l < depth - 1, F32 if precise else BF16,
            HEAD_DIM ** -0.5 * (LOG2E if log2_scores else 1.0))
        kv = (k32, v32)
        seq = lambda a: a.reshape(b, t, a.shape[-1])
        pick = lambda hl: tuple(w[l] for w in hl)
        ya, lru_h, lru_buf = _lru(seq(xbgb), lru_h0[l], lru_buf0[l], prm['conv_w'][l], prm['conv_b'][l],
                                  pick(prm['lru_wa_hl']), prm['lru_ba'][l], pick(prm['lru_wx_hl']), prm['lru_bx'][l],
                                  prm['lru_lambda'][l], precise, F32)
        lf_s = seq(lf)
        cum, cumt = _cum(lf_s, LOG2E if log2_scores else 1.0)
        yb = attend(l, seq(q16), seq(k16), seq(v16), cum, cumt)
        yc, r_out = _retention(seq(rqkv), seq(rg), ret_r0[l], prm['ret_norm_w'][l], prm['g128'], ret_chunk, precise, F32)
        tokv = lambda a: a.reshape(bv, tv, a.shape[-1])
        x1, h2, cmb_t = _out_proj(l, tokv(ya), tokv(yb), tokv(yc), x, m, per_token, prm['w_out_hl'], prm['norm2_w'],
                                  prm['rw_hi'], prm['rw_lo'], prm['router_bias'], tm, True)
        cmb = cmb_t.transpose(0, 2, 1)
        x = _moe(l, h2, cmb, x1, m, per_token, prm['w_gate16'], prm['w_up16'], prm['w_down16'], tm_moe)
        logfs.append(lf_s[:, :, :FOX_HEADS])
        lru_hs.append(lru_h[:, 0, :])
        lru_bufs.append(lru_buf)
        r4 = r_out.reshape(b, npair, 2, HEAD_DIM, 2, HEAD_DIM)
        rets.append(jnp.stack([r4[:, :, 0, :, 0, :], r4[:, :, 1, :, 1, :]], axis=2).reshape(
            b, RET_HEADS, HEAD_DIM, HEAD_DIM))
    k_out = kv[0].reshape(depth, b, t, FOX_HEADS, HEAD_DIM)
    v_out = kv[1].reshape(depth, b, t, FOX_HEADS, HEAD_DIM)
    return x, (k_out, v_out, jnp.stack(logfs), jnp.stack(lru_hs), jnp.stack(lru_bufs), jnp.stack(rets))


def _pair_state(r):
    depth, b, h, d, _ = r.shape
    r6 = r.reshape(depth, b, h // 2, 2, d, d)
    eye = jnp.eye(2, dtype=r.dtype)
    out = r6[:, :, :, :, :, None, :] * eye[None, None, None, :, None, :, None]
    return out.reshape(depth, b, h // 2, 2 * d, 2 * d)


def kernel(x_prompt, x_sample, cache_k, cache_v, cache_logf, state_lru_h, state_lru_conv, state_ret, page_table,
           c_prompt, c_sample, w_in, w_out, conv_w, conv_b, lru_wa, lru_ba, lru_wx, lru_bx, lru_lambda, q_norm_w,
           k_norm_w, forget_bias, ret_norm_w, norm1_w, norm2_w, w_ada, b_ada, router_w, router_bias, w_gate, w_up,
           w_down):
    depth, d, _ = w_in.shape
    bp = x_prompt.shape[0]
    bs, ts, _ = x_sample.shape
    prm = _prep(w_in=w_in, w_out=w_out, conv_w=conv_w, conv_b=conv_b, lru_wa=lru_wa, lru_ba=lru_ba, lru_wx=lru_wx,
                lru_bx=lru_bx, lru_lambda=lru_lambda, q_norm_w=q_norm_w, k_norm_w=k_norm_w, forget_bias=forget_bias,
                ret_norm_w=ret_norm_w, norm1_w=norm1_w, norm2_w=norm2_w, router_w=router_w, router_bias=router_bias,
                w_gate=w_gate, w_up=w_up, w_down=w_down)

    mod = _ada(jnp.concatenate([c_prompt, c_sample], axis=0), w_ada, b_ada)
    mod_p = mod[:, :bp].reshape(depth, bp, 1, 6 * d)
    mod_s = jnp.repeat(mod[:, bp:], ts, axis=1).reshape(depth, 1, bs * ts, 6 * d)
    return _run(prm, mod_p, mod_s, x_prompt, x_sample, cache_k, cache_v, cache_logf, state_lru_h, state_lru_conv,
                state_ret, page_table)


def _prep(w_in, w_out, conv_w, conv_b, lru_wa, lru_ba, lru_wx, lru_bx, lru_lambda, q_norm_w, k_norm_w, forget_bias,
          ret_norm_w, norm1_w, norm2_w, router_w, router_bias, w_gate, w_up, w_down, **unused):
    depth, d, n_in = w_in.shape
    ff0 = 2 * LRU_WIDTH + 3 * FOX_WIDTH
    w_in_p = jnp.concatenate(
        [w_in[:, :, :ff0], w_in[:, :, ff0 + FOX_HEADS:], w_in[:, :, ff0:ff0 + FOX_HEADS],
         jnp.zeros((depth, d, N_PROJ - n_in), w_in.dtype)], axis=-1)
    rw_hi = router_w.T.astype(BF16)
    rw_lo = (router_w.T - rw_hi.astype(F32)).astype(BF16)
    return dict(
        w_in_hl=_split2(w_in_p),
        norm1_w=norm1_w.reshape(depth, 1, d),
        norm2_w=norm2_w.reshape(depth, 1, d),
        q_norm_w=jnp.tile(q_norm_w, (1, FOX_HEADS)).reshape(depth, 1, FOX_WIDTH),
        k_norm_w=jnp.tile(k_norm_w, (1, FOX_HEADS)).reshape(depth, 1, FOX_WIDTH),
        forget_bias=jnp.pad(forget_bias, ((0, 0), (0, LANES - FOX_HEADS))).reshape(depth, 1, LANES),
        g512=_group_mean_matrix(FOX_WIDTH),
        g128=_group_mean_matrix(LANES),
        conv_w=conv_w,
        conv_b=conv_b.reshape(depth, 1, LRU_WIDTH),
        lru_wa_hl=_split2(_block_diag(lru_wa)),
        lru_wx_hl=_split2(_block_diag(lru_wx)),
        lru_ba=lru_ba.reshape(depth, 1, LRU_WIDTH),
        lru_bx=lru_bx.reshape(depth, 1, LRU_WIDTH),
        lru_lambda=lru_lambda.reshape(depth, 1, LRU_WIDTH),
        ret_norm_w=ret_norm_w.reshape(depth, 1, RET_WIDTH),
        w_out_hl=_split2(w_out),
        rw_hi=rw_hi, rw_lo=rw_lo,
        router_bias=router_bias.reshape(N_EXPERTS, 1),
        w_gate16=w_gate.astype(BF16), w_up16=w_up.astype(BF16), w_down16=w_down.astype(BF16),
    )


def _run(prm, mod_p, mod_s, x_prompt, x_sample, cache_k, cache_v, cache_logf, state_lru_h, state_lru_conv, state_ret,
         page_table):
    depth = prm['w_in_hl'][0].shape[0]
    bp, tp, d = x_prompt.shape
    bs, ts, _ = x_sample.shape
    n_pages = page_table.shape[1]
    past = n_pages * cache_k.shape[2]

    npair = RET_HEADS // 2
    zeros = lambda *s: jnp.zeros(s, F32)
    tq = min(tp, 512)
    y_prompt, outs_p = _trunk(
        x_prompt, mod_p, False, (bp, tp), np.arange(tp),
        zeros(depth, bp, 1, LRU_WIDTH), zeros(depth, bp, CONV_W - 1, LRU_WIDTH),
        zeros(depth, bp, npair, LANES, LANES), min(tp, 128),
        lambda l, q, k, v, cum, cumt: _fox_prompt(q, k, v, cumt, tq, F32),
        prm, min(tp, 512), min(tp, 1024), False, True)

    pp = 8 if n_pages % 8 == 0 else 1

    def attend_sample(l, q, k, v, cum, cumt):
        bias = _suffix_bias(l, cache_logf, page_table)
        return _fox_decode(l, q, k, v, cum, cache_k, cache_v, bias, page_table, pp, F32)

    n_s = bs * ts
    y_s, outs_s = _trunk(
        x_sample.reshape(1, n_s, d), mod_s, True, (bs, ts), past + np.tile(np.arange(ts), bs),
        state_lru_h.reshape(depth, bs, 1, LRU_WIDTH), state_lru_conv, _pair_state(state_ret), ts,
        attend_sample, prm, n_s, n_s, True, False)
    return (y_prompt, y_s.reshape(bs, ts, d)) + tuple(outs_p) + tuple(outs_s)
```

```python
import functools
import math

import jax
import jax.numpy as jnp
import numpy as np
from jax import lax
from jax.experimental import pallas as pl
from jax.experimental.pallas import tpu as pltpu

F32 = jnp.float32
BF16 = jnp.bfloat16

HEAD_DIM = 64
LRU_WIDTH = 256
LRU_BLOCKS = 4
CONV_W = 4
LRU_C = 8.0
FOX_WIDTH = 512
FOX_HEADS = 8
RET_WIDTH = 256
RET_HEADS = 4
ROPE_BASE = 10000.0
N_EXPERTS = 16
N_GROUPS = 4
EXPERTS_PER_GROUP = 4
EPS = 1e-6
LANES = 128
LOG2E = math.log2(math.e)
N_PROJ = 3200

VMEM_LIMIT = 56 * 1024 * 1024


def _cp(sem):
    return pltpu.CompilerParams(dimension_semantics=sem, vmem_limit_bytes=VMEM_LIMIT)


def _dot(a, b):
    return jnp.dot(a, b, preferred_element_type=F32)


def _dot_nt(a, b):
    return lax.dot_general(a, b, (((1,), (1,)), ((), ())), preferred_element_type=F32)


def _round_bf16(x):
    bits = lax.bitcast_convert_type(x, jnp.uint32)
    bits = (bits + jnp.uint32(0x7FFF) + ((bits >> 16) & jnp.uint32(1))) & jnp.uint32(0xFFFF0000)
    return lax.bitcast_convert_type(bits, F32)


def _split2(x):
    hi = _round_bf16(x)
    return hi.astype(BF16), (x - hi).astype(BF16)


def _split3(x):
    hi = _round_bf16(x)
    r = x - hi
    mid = _round_bf16(r)
    return hi.astype(BF16), mid.astype(BF16), (r - mid).astype(BF16)


def _dot_f32(a, b):
    ah, al = _split2(a)
    bh, bl = _split2(b)
    return _dot(ah, bh) + (_dot(ah, bl) + _dot(al, bh))


def _dot_exact_rhs(a, m):
    hi, mid, lo = _split3(a)
    return _dot(hi, m) + (_dot(mid, m) + _dot(lo, m))


def _lhs(a, precise):
    return _split2(a.astype(F32)) if precise else (a.astype(BF16),)


def _mm(lhs, w_refs, cols=None):
    sl = (lambda r: r[...]) if cols is None else (lambda r: r[cols[0], cols[1]])
    if len(w_refs) == 1:
        return _dot(lhs[0], sl(w_refs[0]))
    wh, wl = sl(w_refs[0]), sl(w_refs[1])
    return _dot(lhs[0], wh) + (_dot(lhs[0], wl) + _dot(lhs[1], wh))


def _dx(a, b, precise, nt=False):
    f = _dot_nt if nt else _dot
    if not precise:
        return f(a.astype(BF16), b.astype(BF16))
    ah, al = _split2(a.astype(F32))
    bh, bl = _split2(b.astype(F32))
    return f(ah, bh) + (f(ah, bl) + f(al, bh))


def _idiv(x, n):
    assert n & (n - 1) == 0
    return lax.shift_right_logical(x, n.bit_length() - 1)


def _sigmoid(x):
    return 1.0 / (1.0 + jnp.exp(-x))


def _silu(x):
    return x * _sigmoid(x)


def _softplus(x):
    return jnp.maximum(x, 0.0) + jnp.log1p(jnp.exp(-jnp.abs(x)))


def _gelu_tanh(x):
    c = math.sqrt(2.0 / math.pi)
    return 0.5 * x * (1.0 + jnp.tanh(c * (x + 0.044715 * (x * x * x))))


def _rms(x, w):
    ms = jnp.mean(x * x, axis=-1, keepdims=True)
    return x * lax.rsqrt(ms + EPS) * w


def _ada_kernel(c_ref, w_ref, b_ref, o_ref):
    s = _silu(c_ref[...])
    o_ref[...] = _dot_f32(s, w_ref[...]) + b_ref[...]


def _ada(c_all, w_ada, b_ada):
    depth, d, n6 = w_ada.shape
    bc = c_all.shape[0]
    tn = 1024
    return pl.pallas_call(
        _ada_kernel,
        out_shape=jax.ShapeDtypeStruct((depth, bc, n6), F32),
        grid=(depth, n6 // tn),
        in_specs=[
            pl.BlockSpec((bc, d), lambda l, j: (0, 0)),
            pl.BlockSpec((None, d, tn), lambda l, j: (l, 0, j)),
            pl.BlockSpec((None, 1, tn), lambda l, j: (l, 0, j)),
        ],
        out_specs=pl.BlockSpec((None, bc, tn), lambda l, j: (l, 0, j)),
        compiler_params=_cp(("parallel", "parallel")),
        name="ada_mod",
    )(c_all, w_ada, b_ada.reshape(depth, 1, n6))


def _proj_kernel(has_alias, precise, q_scale, x_ref, sh_ref, sc_ref, n1_ref, qw_ref, kw_ref, fb_ref, g_ref,
                 cos_ref, sin_ref, *rest):
    nw = 2 if precise else 1
    w_refs, rest = rest[:nw], rest[nw:]
    if has_alias:
        rest = rest[2:]
    xbgb_ref, q_ref, k32_ref, k16_ref, v32_ref, v16_ref, rqkv_ref, rg_ref, lf_ref = rest
    x = x_ref[...]
    h = _rms(x, n1_ref[...]) * (1.0 + sc_ref[...]) + sh_ref[...]
    hl = _lhs(h, precise)
    g = g_ref[...]

    def seg(lo, hi):
        return _mm(hl, w_refs, (slice(None), slice(lo, hi)))

    xbgb_ref[...] = seg(0, 512)

    def head_norm(t, w):
        hi, lo = _split2(t * t)
        ms = _dot(hi, g) + _dot(lo, g)
        return t * lax.rsqrt(ms + EPS) * w

    q = head_norm(seg(512, 1024), qw_ref[...])
    q_ref[...] = (q * q_scale).astype(q_ref.dtype)

    def store_layer(ref, val):
        if has_alias:
            ref[...] = val
        else:
            ref[0] = val
            ref[1:] = jnp.zeros((ref.shape[0] - 1,) + val.shape, val.dtype)

    k = head_norm(seg(1024, 1536), kw_ref[...])
    store_layer(k32_ref, k)
    k16_ref[...] = k.astype(BF16)
    v = seg(1536, 2048)
    store_layer(v32_ref, v)
    v16_ref[...] = v.astype(BF16)

    cos = cos_ref[...]
    sin = sin_ref[...]
    first_half = (lax.broadcasted_iota(jnp.int32, cos.shape, 1) & 32) == 0

    def rope(t):
        rot = jnp.where(first_half, -pltpu.roll(t, RET_WIDTH - 32, 1), pltpu.roll(t, 32, 1))
        return t * cos + rot * sin

    rdt = rqkv_ref.dtype
    rqkv_ref[:, 0:256] = rope(seg(2048, 2304)).astype(rdt)
    rqkv_ref[:, 256:512] = (rope(seg(2304, 2560)) * (HEAD_DIM ** -0.5)).astype(rdt)
    rqkv_ref[:, 512:768] = seg(2560, 2816).astype(rdt)
    rg_ref[...] = seg(2816, 3072)

    ff = seg(3072, 3200) + fb_ref[...]
    lf_ref[...] = -_softplus(-ff)


def _proj(layer, x, mod, per_token, n1, w_in_hl, qw, kw, fb, gmat, cos_t, sin_t, kv_prev, tm, precise, adt, q_scale):
    b, t, d = x.shape
    depth = w_in_hl[0].shape[0]
    nt = t // tm
    tmm = tm if per_token else 1
    mi = (lambda bi, ti: ti) if per_token else (lambda bi, ti: 0)
    has_alias = kv_prev is not None
    w_list = list(w_in_hl) if precise else [w_in_hl[0]]
    in_specs = [
        pl.BlockSpec((None, tm, d), lambda bi, ti: (bi, ti, 0)),
        pl.BlockSpec((None, tmm, d), lambda bi, ti: (bi, mi(bi, ti), 0)),
        pl.BlockSpec((None, tmm, d), lambda bi, ti: (bi, mi(bi, ti), 1)),
        pl.BlockSpec((None, 1, d), lambda bi, ti: (layer, 0, 0)),
        pl.BlockSpec((None, 1, FOX_WIDTH), lambda bi, ti: (layer, 0, 0)),
        pl.BlockSpec((None, 1, FOX_WIDTH), lambda bi, ti: (layer, 0, 0)),
        pl.BlockSpec((None, 1, LANES), lambda bi, ti: (layer, 0, 0)),
        pl.BlockSpec((FOX_WIDTH, FOX_WIDTH), lambda bi, ti: (0, 0)),
        pl.BlockSpec((tm, RET_WIDTH), lambda bi, ti: (ti, 0)),
        pl.BlockSpec((tm, RET_WIDTH), lambda bi, ti: (ti, 0)),
    ] + [pl.BlockSpec((None, d, N_PROJ), lambda bi, ti: (layer, 0, 0), pipeline_mode=pl.Buffered(1)) for _ in w_list]
    args = [x, mod, mod, n1, qw, kw, fb, gmat, cos_t, sin_t] + w_list
    aliases = {}
    if has_alias:
        in_specs += [pl.BlockSpec(memory_space=pl.ANY), pl.BlockSpec(memory_space=pl.ANY)]
        aliases = {len(args): 2, len(args) + 1: 4}
        args += list(kv_prev)
    tok = lambda w, dt: jax.ShapeDtypeStruct((b, t, w), dt)
    kv_shape = jax.ShapeDtypeStruct((depth, b, t, FOX_WIDTH), F32)
    tok_spec = lambda w: pl.BlockSpec((None, tm, w), lambda bi, ti: (bi, ti, 0))
    if has_alias:
        kv_spec = pl.BlockSpec((None, None, tm, FOX_WIDTH), lambda bi, ti: (layer, bi, ti, 0))
    else:
        assert layer == 0
        kv_spec = pl.BlockSpec((depth, None, tm, FOX_WIDTH), lambda bi, ti: (0, bi, ti, 0))
    return pl.pallas_call(
        functools.partial(_proj_kernel, has_alias, precise, q_scale),
        out_shape=(tok(512, F32), tok(FOX_WIDTH, adt), kv_shape, tok(FOX_WIDTH, BF16), kv_shape,
                   tok(FOX_WIDTH, BF16), tok(768, adt), tok(RET_WIDTH, F32), tok(LANES, F32)),
        grid=(b, nt),
        in_specs=in_specs,
        out_specs=(tok_spec(512), tok_spec(FOX_WIDTH), kv_spec, tok_spec(FOX_WIDTH), kv_spec,
                   tok_spec(FOX_WIDTH), tok_spec(768), tok_spec(RET_WIDTH), tok_spec(LANES)),
        input_output_aliases=aliases,
        compiler_params=_cp(("parallel", "parallel")),
        name="norm_proj",
    )(*args)


def _lru_kernel(ch, precise, x_ref, h0_ref, buf_ref, cw_ref, cb_ref, ba_ref, bx_ref, lam_ref, *rest):
    nw = 2 if precise else 1
    wa_refs, wx_refs, rest = rest[:nw], rest[nw:2 * nw], rest[2 * nw:]
    y_ref, hl_ref, bo_ref, xpad, a_s, b_s, h_s, hc = rest
    ci = pl.program_id(1)

    @pl.when(ci == 0)
    def _():
        xpad[0:8, :] = jnp.zeros((8, LRU_WIDTH), F32)
        xpad[5:8, :] = buf_ref[...]
        hc[...] = jnp.broadcast_to(h0_ref[...], (8, LRU_WIDTH))

    xb = x_ref[:, 0:LRU_WIDTH]
    gb = x_ref[:, LRU_WIDTH:2 * LRU_WIDTH]
    xpad[8:8 + ch, :] = xb
    xc = cw_ref[0:1, :] * xpad[5:5 + ch, :]
    xc = xc + cw_ref[1:2, :] * xpad[6:6 + ch, :]
    xc = xc + cw_ref[2:3, :] * xpad[7:7 + ch, :]
    xc = xc + cw_ref[3:4, :] * xb
    xc = xc + cb_ref[...]
    xl = _lhs(xc, precise)
    r = _sigmoid(_mm(xl, wa_refs) + ba_ref[...])
    i = _sigmoid(_mm(xl, wx_refs) + bx_ref[...])
    log_a = (-LRU_C) * r * _softplus(-lam_ref[...])
    a_s[...] = jnp.exp(log_a)
    th = jnp.tanh(log_a)
    b_s[...] = jnp.sqrt(-2.0 * th / (1.0 - th)) * i * xc

    row = lax.broadcasted_iota(jnp.int32, (8, LRU_WIDTH), 0)

    def blk(j, h):
        r0 = pl.multiple_of(j * 8, 8)
        a = a_s[pl.ds(r0, 8), :]
        bv = b_s[pl.ds(r0, 8), :]
        for s in (1, 2, 4):
            keep = row >= s
            a_sh = jnp.where(keep, pltpu.roll(a, s, 0), 1.0)
            b_sh = jnp.where(keep, pltpu.roll(bv, s, 0), 0.0)
            bv = a * b_sh + bv
            a = a * a_sh
        hb = a * h + bv
        h_s[pl.ds(r0, 8), :] = hb
        return jnp.broadcast_to(hb[7:8, :], (8, LRU_WIDTH))

    h_end = lax.fori_loop(0, ch // 8, blk, hc[...])
    hc[...] = h_end
    y_ref[...] = (h_s[...] * _gelu_tanh(gb)).astype(y_ref.dtype)
    tail = xpad[ch:ch + 8, :]
    xpad[0:8, :] = tail

    @pl.when(ci == pl.num_programs(1) - 1)
    def _():
        hl_ref[...] = h_end[0:1, :]
        bo_ref[...] = tail[5:8, :]


def _lru(xbgb, h0, buf, cw, cb, wa_hl, ba, wx_hl, bx, lam, precise, ydt):
    b, t, _ = xbgb.shape
    ch = min(t, 256)
    w = LRU_WIDTH
    vec = lambda: pl.BlockSpec((1, w), lambda bi, ci: (0, 0))
    mats = (list(wa_hl) + list(wx_hl)) if precise else [wa_hl[0], wx_hl[0]]
    return pl.pallas_call(
        functools.partial(_lru_kernel, ch, precise),
        out_shape=(jax.ShapeDtypeStruct((b, t, w), ydt), jax.ShapeDtypeStruct((b, 1, w), F32),
                   jax.ShapeDtypeStruct((b, CONV_W - 1, w), F32)),
        grid=(b, t // ch),
        in_specs=[
            pl.BlockSpec((None, ch, 2 * w), lambda bi, ci: (bi, ci, 0)),
            pl.BlockSpec((None, 1, w), lambda bi, ci: (bi, 0, 0)),
            pl.BlockSpec((None, CONV_W - 1, w), lambda bi, ci: (bi, 0, 0)),
            pl.BlockSpec((CONV_W, w), lambda bi, ci: (0, 0)),
            vec(), vec(), vec(), vec(),
        ] + [pl.BlockSpec((w, w), lambda bi, ci: (0, 0)) for _ in mats],
        out_specs=(pl.BlockSpec((None, ch, w), lambda bi, ci: (bi, ci, 0)),
                   pl.BlockSpec((None, 1, w), lambda bi, ci: (bi, 0, 0)),
                   pl.BlockSpec((None, CONV_W - 1, w), lambda bi, ci: (bi, 0, 0))),
        scratch_shapes=[pltpu.VMEM((ch + 8, w), F32), pltpu.VMEM((ch, w), F32), pltpu.VMEM((ch, w), F32),
                        pltpu.VMEM((ch, w), F32), pltpu.VMEM((8, w), F32)],
        compiler_params=_cp(("parallel", "arbitrary")),
        name="rglru",
    )(xbgb, h0, buf, cw, cb, ba, bx, lam, *mats)


def _cum_kernel(t, tc, tscale, lf_ref, cum_ref, cumt_ref, carry):
    ci = pl.program_id(1)

    @pl.when(ci == 0)
    def _():
        carry[...] = jnp.zeros_like(carry)

    lf = lf_ref[...]
    if t < tc:
        lf = jnp.concatenate([lf, jnp.zeros((tc - t, LANES), F32)], axis=0)
    r = lax.broadcasted_iota(jnp.int32, (tc, tc), 0)
    c = lax.broadcasted_iota(jnp.int32, (tc, tc), 1)
    tri = jnp.where(c <= r, 1.0, 0.0).astype(BF16)
    hi, mid, lo = _split3(lf)
    cum = _dot(tri, hi) + (_dot(tri, mid) + _dot(tri, lo)) + carry[0:1, :]
    carry[...] = jnp.broadcast_to(cum[tc - 1:tc, :], carry.shape)
    cum_ref[...] = cum[0:t, :] if t < tc else cum
    cumt_ref[...] = cum.T[0:FOX_HEADS, :] * tscale


def _cum(lf, tscale):
    b, t, _ = lf.shape
    tc = 128 if t < 128 else min(t, 512)
    tb = min(t, tc)
    nt = max(t // tc, 1)
    return pl.pallas_call(
        functools.partial(_cum_kernel, tb, tc, tscale),
        out_shape=(jax.ShapeDtypeStruct((b, t, LANES), F32), jax.ShapeDtypeStruct((b, FOX_HEADS, nt * tc), F32)),
        grid=(b, nt),
        in_specs=[pl.BlockSpec((None, tb, LANES), lambda bi, ci: (bi, ci, 0))],
        out_specs=(pl.BlockSpec((None, tb, LANES), lambda bi, ci: (bi, ci, 0)),
                   pl.BlockSpec((None, FOX_HEADS, tc), lambda bi, ci: (bi, 0, ci))),
        scratch_shapes=[pltpu.VMEM((8, LANES), F32)],
        compiler_params=_cp(("parallel", "arbitrary")),
        name="logf_cumsum",
    )(lf)


def _fox_kernel(tq, ts, q_ref, k_ref, v_ref, cumt_ref, o_ref, m_s, l_s, acc_s):
    hp = pl.program_id(1)
    qi = pl.program_id(2)
    n_strips = tq // ts
    q = q_ref[...]
    lane = lax.broadcasted_iota(jnp.int32, (tq, LANES), 1)
    low = lane < HEAD_DIM
    zero = jnp.zeros_like(q)
    qm = (jnp.where(low, q, zero), jnp.where(low, zero, q))
    m_s[...] = jnp.full_like(m_s, -jnp.inf)
    l_s[...] = jnp.zeros_like(l_s)
    acc_s[...] = jnp.zeros_like(acc_s)

    def strip(hh, r, k, v, ck, diagonal, state):
        ncol = (r + 1) * ts if diagonal else tq
        s = _dot_nt(qm[hh][r * ts:(r + 1) * ts, :], k[0:ncol, :]) - ck[:, 0:ncol]
        if diagonal:
            rr = lax.broadcasted_iota(jnp.int32, (ts, ncol), 0)
            cc = lax.broadcasted_iota(jnp.int32, (ts, ncol), 1)
            s = jnp.where(cc <= rr + r * ts, s, -jnp.inf)
        m_prev, l_prev, acc_prev = state
        m_new = jnp.maximum(m_prev, jnp.max(s, axis=-1, keepdims=True))
        alpha = jnp.exp2(m_prev - m_new)
        p = jnp.exp2(s - jnp.concatenate([m_new] * (ncol // LANES), axis=1))
        l_new = alpha * l_prev + jnp.sum(p, axis=-1, keepdims=True)
        acc_new = alpha * acc_prev + _dot(p.astype(BF16), v[0:ncol, :])
        return m_new, l_new, acc_new

    def tile(ki, diagonal):
        k0 = pl.multiple_of(ki * tq, tq)
        k = k_ref[pl.ds(k0, tq), :]
        v = v_ref[pl.ds(k0, tq), :]
        chains = [(hh, r) for hh in range(2) for r in range(n_strips)]
        rows = lambda r: slice(r * ts, (r + 1) * ts)
        states = [(m_s[hh, rows(r), :], l_s[hh, rows(r), :], acc_s[hh, rows(r), :]) for hh, r in chains]
        cks = [cumt_ref[pl.ds(2 * hp + hh, 1), pl.ds(k0, tq)] for hh in range(2)]
        new = [strip(hh, r, k, v, cks[hh], diagonal, st) for (hh, r), st in zip(chains, states)]
        for (hh, r), (m_new, l_new, acc_new) in zip(chains, new):
            m_s[hh, rows(r), :] = m_new
            l_s[hh, rows(r), :] = l_new
            acc_s[hh, rows(r), :] = acc_new

    def body(ki, c):
        tile(ki, False)
        return c

    lax.fori_loop(0, qi, body, 0)
    tile(qi, True)
    o_ref[...] = jnp.where(low, acc_s[0] / l_s[0], acc_s[1] / l_s[1]).astype(o_ref.dtype)


def _fox_prompt(q16, k16, v16, cumt2, tq, ydt):
    b, t, _ = q16.shape
    npair = FOX_HEADS // 2
    ts = min(tq, LANES)
    return pl.pallas_call(
        functools.partial(_fox_kernel, tq, ts),
        out_shape=jax.ShapeDtypeStruct((b, t, FOX_WIDTH), ydt),
        grid=(b, npair, t // tq),
        in_specs=[
            pl.BlockSpec((None, tq, LANES), lambda bi, hp, qi: (bi, qi, hp)),
            pl.BlockSpec((None, t, LANES), lambda bi, hp, qi: (bi, 0, hp)),
            pl.BlockSpec((None, t, LANES), lambda bi, hp, qi: (bi, 0, hp)),
            pl.BlockSpec((None, FOX_HEADS, t), lambda bi, hp, qi: (bi, 0, 0)),
        ],
        out_specs=pl.BlockSpec((None, tq, LANES), lambda bi, hp, qi: (bi, qi, hp)),
        scratch_shapes=[pltpu.VMEM((2, tq, LANES), F32), pltpu.VMEM((2, tq, LANES), F32),
                        pltpu.VMEM((2, tq, LANES), F32)],
        compiler_params=_cp(("parallel", "parallel", "arbitrary")),
        name="fox_prompt",
    )(q16, k16, v16, cumt2)


def _suffix_kernel(layer, n_pages, nh, ps, pt_ref, lf_hbm, later_ref, o_ref, xbuf, sem):
    bi = pl.program_id(0)

    def copy(p):
        return pltpu.make_async_copy(lf_hbm.at[layer, pt_ref[bi, p]], xbuf.at[p], sem)

    def start(p, c):
        copy(p).start()
        return c

    def wait(p, c):
        copy(p).wait()
        return c

    lax.fori_loop(0, n_pages, start, 0)
    lax.fori_loop(0, n_pages, wait, 0)
    x = xbuf[...].reshape(n_pages * nh, ps)
    r = lax.broadcasted_iota(jnp.int32, (ps, ps), 0)
    c = lax.broadcasted_iota(jnp.int32, (ps, ps), 1)
    after = jnp.where(r > c, 1.0, 0.0).astype(BF16)
    within = _dot_exact_rhs(x, after)
    tot = _dot_exact_rhs(x, jnp.ones((ps, ps), BF16))
    hi, mid, lo = _split3(tot)
    later = later_ref[...]
    carry = _dot(later, hi) + (_dot(later, mid) + _dot(later, lo))
    o_ref[...] = (within + carry).reshape(n_pages, nh, ps)


def _suffix_bias(layer, cache_logf, page_table):
    depth, n_pool, ps, nh = cache_logf.shape
    b, n_pages = page_table.shape
    lf = cache_logf.transpose(0, 1, 3, 2)
    idx = np.arange(n_pages * nh)
    pg, head = idx // nh, idx % nh
    later = jnp.asarray((head[:, None] == head[None, :]) & (pg[None, :] > pg[:, None]), BF16)
    grid_spec = pltpu.PrefetchScalarGridSpec(
        num_scalar_prefetch=1,
        grid=(b,),
        in_specs=[
            pl.BlockSpec(memory_space=pl.ANY),
            pl.BlockSpec((n_pages * nh, n_pages * nh), lambda bi, pt: (0, 0)),
        ],
        out_specs=pl.BlockSpec((None, n_pages, nh, ps), lambda bi, pt: (bi, 0, 0, 0)),
        scratch_shapes=[pltpu.VMEM((n_pages, nh, ps), F32), pltpu.SemaphoreType.DMA(())],
    )
    return pl.pallas_call(
        functools.partial(_suffix_kernel, layer, n_pages, nh, ps),
        out_shape=jax.ShapeDtypeStruct((b, n_pages, nh, ps), F32),
        grid_spec=grid_spec,
        compiler_params=_cp(("arbitrary",)),
        name="logf_suffix",
    )(page_table, lf, later)


def _decode_kernel(pp, nh, ps, s_new, pt_ref, q_ref, cb_ref, bias_ref, kn_ref, vn_ref, nb_ref, *rest):
    k_refs = rest[:pp]
    v_refs = rest[pp:2 * pp]
    o_ref, m_s, l_s, acc_s = rest[2 * pp:]
    j = pl.program_id(1)
    rows = nh * s_new
    w = nh * HEAD_DIM

    @pl.when(j == 0)
    def _():
        m_s[...] = jnp.full_like(m_s, -jnp.inf)
        l_s[...] = jnp.zeros_like(l_s)
        acc_s[...] = jnp.zeros_like(acc_s)

    q = q_ref[...]
    cb = cb_ref[...]

    def qk(kt):
        s2 = _dot(q, kt)
        return s2[0:rows, :] + s2[rows:2 * rows, :]

    def step(s_list, vt_list):
        m = m_s[...]
        m_new = m
        for s in s_list:
            m_new = jnp.maximum(m_new, jnp.max(s, axis=-1, keepdims=True))
        alpha = jnp.exp(m - m_new)
        l = alpha * l_s[...]
        acc = alpha * acc_s[...]
        for s, vt in zip(s_list, vt_list):
            p = jnp.exp(s - m_new)
            l = l + jnp.sum(p, axis=-1, keepdims=True)
            acc = acc + _dot_nt(p.astype(BF16), vt)
        m_s[...] = m_new
        l_s[...] = l
        acc_s[...] = acc

    s_list, vt_list = [], []
    for r in range(pp):
        kt = k_refs[r][...].reshape(w, ps).astype(BF16)
        bias = bias_ref[r]
        bias = jnp.concatenate([jnp.broadcast_to(bias[h:h + 1, :], (s_new, ps)) for h in range(nh)], axis=0)
        s_list.append(qk(kt) + (bias + cb))
        vt_list.append(v_refs[r][...].reshape(w, ps).astype(BF16))
    step(s_list, vt_list)

    @pl.when(j == pl.num_programs(1) - 1)
    def _():
        rr = lax.broadcasted_iota(jnp.int32, (rows, ps), 0)
        cc = lax.broadcasted_iota(jnp.int32, (rows, ps), 1)
        ok = cc <= (rr & (s_new - 1))
        s = qk(kn_ref[...]) + (cb - nb_ref[...])
        step([jnp.where(ok, s, -jnp.inf)], [vn_ref[...]])
        o = acc_s[...] / l_s[...]
        rh = _idiv(lax.broadcasted_iota(jnp.int32, (rows, w), 0), s_new)
        ch = _idiv(lax.broadcasted_iota(jnp.int32, (rows, w), 1), HEAD_DIM)
        o = jnp.where(rh == ch, o, 0.0)
        out = o[0:s_new, :]
        for h in range(1, nh):
            out = out + o[h * s_new:(h + 1) * s_new, :]
        o_ref[...] = out.astype(o_ref.dtype)


def _fox_decode(layer, q32, k16, v16, cum, cache_k, cache_v, bias, page_table, pp, ydt):
    depth, n_pool, ps, nh, hd = cache_k.shape
    b, s_new, w = q32.shape
    n_pages = page_table.shape[1]
    rows = nh * s_new
    assert s_new & (s_new - 1) == 0 and nh & (nh - 1) == 0 and s_new <= ps
    ckt = cache_k.transpose(0, 1, 3, 4, 2)
    cvt = cache_v.transpose(0, 1, 3, 4, 2)
    q4 = q32.reshape(b, s_new, nh, hd)
    eye = jnp.eye(nh, dtype=F32)
    qbd = (q4.transpose(0, 2, 1, 3)[:, :, :, None, :] * eye[None, :, None, :, None]).reshape(b, rows, w)
    qbd = jnp.concatenate(_split2(qbd), axis=1)
    cum_h = cum[:, :, :nh].transpose(0, 2, 1)
    colb = cum_h.reshape(b, rows, 1)
    newb = jnp.pad(jnp.repeat(cum_h, s_new, axis=1), ((0, 0), (0, 0), (0, ps - s_new)))
    padt = lambda a: jnp.pad(a.transpose(0, 2, 1), ((0, 0), (0, 0), (0, ps - s_new)))
    knt = padt(k16)
    vnt = padt(v16)

    def page_spec(r):
        return pl.BlockSpec((None, None, nh, hd, ps), lambda bi, j, pt: (layer, pt[bi, j * pp + r], 0, 0, 0))

    grid_spec = pltpu.PrefetchScalarGridSpec(
        num_scalar_prefetch=1,
        grid=(b, n_pages // pp),
        in_specs=[
            pl.BlockSpec((None, 2 * rows, w), lambda bi, j, pt: (bi, 0, 0)),
            pl.BlockSpec((None, rows, 1), lambda bi, j, pt: (bi, 0, 0)),
            pl.BlockSpec((None, pp, nh, ps), lambda bi, j, pt: (bi, j, 0, 0)),
            pl.BlockSpec((None, w, ps), lambda bi, j, pt: (bi, 0, 0)),
            pl.BlockSpec((None, w, ps), lambda bi, j, pt: (bi, 0, 0)),
            pl.BlockSpec((None, rows, ps), lambda bi, j, pt: (bi, 0, 0)),
        ] + [page_spec(r) for r in range(pp)] + [page_spec(r) for r in range(pp)],
        out_specs=pl.BlockSpec((None, s_new, w), lambda bi, j, pt: (bi, 0, 0)),
        scratch_shapes=[pltpu.VMEM((rows, 1), F32), pltpu.VMEM((rows, 1), F32), pltpu.VMEM((rows, w), F32)],
    )
    return pl.pallas_call(
        functools.partial(_decode_kernel, pp, nh, ps, s_new),
        out_shape=jax.ShapeDtypeStruct((b, s_new, w), ydt),
        grid_spec=grid_spec,
        compiler_params=_cp(("parallel", "arbitrary")),
        name="fox_decode",
    )(page_table, qbd, colb, bias, knt, vnt, newb, *([ckt] * pp), *([cvt] * pp))


def _ret_kernel(c, n_chunks, precise, q_ref, k_ref, v_ref, rg_ref, r0_ref, dm_ref, qd_ref, kd_ref, cd_ref, nw_ref, g_ref,
                y_ref, ro_ref, r_s):
    cp = max(c, LANES)
    lane = lax.broadcasted_iota(jnp.int32, (c, LANES), 1)
    low = lane < HEAD_DIM
    br = _idiv(lax.broadcasted_iota(jnp.int32, (LANES, LANES), 0), HEAD_DIM)
    bc = _idiv(lax.broadcasted_iota(jnp.int32, (LANES, LANES), 1), HEAD_DIM)
    blockdiag = br == bc
    g = g_ref[...]
    r_s[...] = r0_ref[...]

    def gmean(t):
        hi, lo = _split2(t)
        return _dot(hi, g) + _dot(lo, g)

    def chunk(ci, carry):
        r0 = ci * c if isinstance(ci, int) else pl.multiple_of(ci * c, c)
        q = q_ref[pl.ds(r0, c), :]
        k = k_ref[pl.ds(r0, c), :].astype(F32)
        v = v_ref[pl.ds(r0, c), :].astype(F32)
        if c < cp:
            pad = jnp.zeros((cp - c, LANES), F32)
            k = jnp.concatenate([k, pad], axis=0)
            v = jnp.concatenate([v, pad], axis=0)
        if not precise:
            v = v.astype(BF16)
        rmat = r_s[...]
        zero = jnp.zeros_like(q)
        o = _dx(q, rmat, precise) * qd_ref[...]
        for hh in range(2):
            sel = low if hh == 0 else jnp.logical_not(low)
            s = _dx(jnp.where(sel, q, zero), k, precise, nt=True) * dm_ref[hh]
            oh = _dx(s, v, precise)
            o = o + jnp.where(sel, oh, 0.0)
        rn = _dx((k * kd_ref[...]).T, v, precise)
        r_s[...] = rmat * cd_ref[...] + jnp.where(blockdiag, rn, 0.0)
        mu = gmean(o)
        oc = o - mu
        var = gmean(oc * oc)
        on = oc * lax.rsqrt(var + EPS) * nw_ref[...]
        y_ref[pl.ds(r0, c), :] = (on * _silu(rg_ref[pl.ds(r0, c), :])).astype(y_ref.dtype)
        return carry

    if n_chunks == 1:
        chunk(0, 0)
    else:
        lax.fori_loop(0, n_chunks, chunk, 0)
    ro_ref[...] = r_s[...]


def _ret_tables(c):
    cp = max(c, LANES)
    hidx = np.arange(RET_HEADS, dtype=np.float64)
    log_g = np.log1p(-np.exp2(-5.0 - hidx))
    j = np.arange(c, dtype=np.float64)
    diff = j[:, None] - j[None, :]
    dmask = np.where(diff >= 0, np.exp(log_g[:, None, None] * np.maximum(diff, 0.0)), 0.0)
    dm = np.zeros((RET_HEADS, c, cp))
    dm[:, :, :c] = dmask
    q_dec = np.exp(log_g[:, None] * (j + 1.0))
    k_dec = np.exp(log_g[:, None] * (c - 1.0 - j))
    c_dec = np.exp(log_g * c)
    npair = RET_HEADS // 2
    qd = np.zeros((npair, c, LANES))
    kd = np.zeros((npair, cp, LANES))
    cd = np.zeros((npair, 1, LANES))
    for p in range(npair):
        for hh in range(2):
            sl = slice(hh * HEAD_DIM, (hh + 1) * HEAD_DIM)
            qd[p, :, sl] = q_dec[2 * p + hh][:, None]
            kd[p, :c, sl] = k_dec[2 * p + hh][:, None]
            cd[p, 0, sl] = c_dec[2 * p + hh]
    f = lambda a: jnp.asarray(a, F32)
    return f(dm.reshape(npair, 2, c, cp)), f(qd), f(kd), f(cd)


def _retention(rqkv, rg, r0, nw, gpair, c, precise, ydt):
    b, t, _ = rg.shape
    npair = RET_HEADS // 2
    cp = max(c, LANES)
    dm, qd, kd, cd = _ret_tables(c)
    col = lambda off: pl.BlockSpec((None, t, LANES), lambda bi, p: (bi, 0, off + p))
    return pl.pallas_call(
        functools.partial(_ret_kernel, c, t // c, precise),
        out_shape=(jax.ShapeDtypeStruct((b, t, RET_WIDTH), ydt),
                   jax.ShapeDtypeStruct((b, npair, LANES, LANES), F32)),
        grid=(b, npair),
        in_specs=[
            col(0), col(npair), col(2 * npair),
            pl.BlockSpec((None, t, LANES), lambda bi, p: (bi, 0, p)),
            pl.BlockSpec((None, None, LANES, LANES), lambda bi, p: (bi, p, 0, 0)),
            pl.BlockSpec((None, 2, c, cp), lambda bi, p: (p, 0, 0, 0)),
            pl.BlockSpec((None, c, LANES), lambda bi, p: (p, 0, 0)),
            pl.BlockSpec((None, cp, LANES), lambda bi, p: (p, 0, 0)),
            pl.BlockSpec((None, 1, LANES), lambda bi, p: (p, 0, 0)),
            pl.BlockSpec((1, LANES), lambda bi, p: (0, p)),
            pl.BlockSpec((LANES, LANES), lambda bi, p: (0, 0)),
        ],
        out_specs=(pl.BlockSpec((None, t, LANES), lambda bi, p: (bi, 0, p)),
                   pl.BlockSpec((None, None, LANES, LANES), lambda bi, p: (bi, p, 0, 0))),
        scratch_shapes=[pltpu.VMEM((LANES, LANES), F32)],
        compiler_params=_cp(("parallel", "parallel")),
        name="retention",
    )(rqkv, rqkv, rqkv, rg, r0, dm, qd, kd, cd, nw, gpair)


def _route_rows(logits, bias):
    rows = [logits[e:e + 1, :] for e in range(N_EXPERTS)]
    m = rows[0]
    for r in rows[1:]:
        m = jnp.maximum(m, r)
    ex = [jnp.exp(r - m) for r in rows]
    z = ex[0]
    for e in ex[1:]:
        z = z + e
    score = [e / z for e in ex]
    sel = [score[e] + bias[e:e + 1, :] for e in range(N_EXPERTS)]
    top2, gscore = [], []
    for g in range(N_GROUPS):
        v = sel[g * EXPERTS_PER_GROUP:(g + 1) * EXPERTS_PER_GROUP]
        gs = None
        for i in range(EXPERTS_PER_GROUP):
            rank = None
            for j in range(EXPERTS_PER_GROUP):
                if j == i:
                    continue
                ahead = (v[j] >= v[i]) if j < i else (v[j] > v[i])
                ahead = jnp.where(ahead, 1.0, 0.0)
                rank = ahead if rank is None else rank + ahead
            keep = rank < 1.5
            top2.append(keep)
            term = jnp.where(keep, v[i], 0.0)
            gs = term if gs is None else gs + term
        gscore.append(gs)
    out = []
    for g in range(N_GROUPS):
        win = None
        for j in range(N_GROUPS):
            if j == g:
                continue
            c = (gscore[g] > gscore[j]) if j < g else (gscore[g] >= gscore[j])
            win = c if win is None else (win & c)
        for i in range(EXPERTS_PER_GROUP):
            e = g * EXPERTS_PER_GROUP + i
            out.append(jnp.where(win & top2[e], score[e], 0.0))
    tot = out[0]
    for o in out[1:]:
        tot = tot + o
    return jnp.concatenate([o / tot for o in out], axis=0)


def _out_kernel(precise, ya_ref, yb_ref, yc_ref, x_ref, g1_ref, sh_ref, sc_ref, n2_ref, rwh_ref, rwl_ref, rb_ref,
                *rest):
    nw = 2 if precise else 1
    w_refs, (x1_ref, h2_ref, cmb_ref) = rest[:nw], rest[nw:]
    mix = _mm(_lhs(ya_ref[...], precise), w_refs, (slice(0, 256), slice(None)))
    mix = mix + _mm(_lhs(yb_ref[...], precise), w_refs, (slice(256, 768), slice(None)))
    mix = mix + _mm(_lhs(yc_ref[...], precise), w_refs, (slice(768, 1024), slice(None)))
    x1 = x_ref[...] + g1_ref[...] * mix
    x1_ref[...] = x1
    h2 = _rms(x1, n2_ref[...]) * (1.0 + sc_ref[...]) + sh_ref[...]
    hi, lo = _split2(h2)
    h2_ref[...] = h2
    rwh = rwh_ref[...]
    logits = _dot_nt(rwh, hi) + (_dot_nt(rwh, lo) + _dot_nt(rwl_ref[...], hi))
    cmb_ref[...] = _route_rows(logits, rb_ref[...])


def _out_proj(layer, ya, yb, yc, x, mod, per_token, w_out_hl, n2, rwh, rwl, rb, tm, precise):
    b, t, d = x.shape
    tmm = tm if per_token else 1
    mi = (lambda bi, ti: ti) if per_token else (lambda bi, ti: 0)
    tok = lambda w: pl.BlockSpec((None, tm, w), lambda bi, ti: (bi, ti, 0))
    modspec = lambda col: pl.BlockSpec((None, tmm, d), lambda bi, ti: (bi, mi(bi, ti), col))
    w_list = list(w_out_hl) if precise else [w_out_hl[0]]
    return pl.pallas_call(
        functools.partial(_out_kernel, precise),
        out_shape=(jax.ShapeDtypeStruct((b, t, d), F32), jax.ShapeDtypeStruct((b, t, d), F32),
                   jax.ShapeDtypeStruct((b, N_EXPERTS, t), F32)),
        grid=(b, t // tm),
        in_specs=[
            tok(LRU_WIDTH), tok(FOX_WIDTH), tok(RET_WIDTH), tok(d),
            modspec(2), modspec(3), modspec(4),
            pl.BlockSpec((None, 1, d), lambda bi, ti: (layer, 0, 0)),
            pl.BlockSpec((N_EXPERTS, d), lambda bi, ti: (0, 0)),
            pl.BlockSpec((N_EXPERTS, d), lambda bi, ti: (0, 0)),
            pl.BlockSpec((N_EXPERTS, 1), lambda bi, ti: (0, 0)),
        ] + [pl.BlockSpec((None, d, d), lambda bi, ti: (layer, 0, 0)) for _ in w_list],
        out_specs=(tok(d), tok(d), pl.BlockSpec((None, N_EXPERTS, tm), lambda bi, ti: (bi, 0, ti))),
        compiler_params=_cp(("parallel", "parallel")),
        name="out_proj_route",
    )(ya, yb, yc, x, mod, mod, mod, n2, rwh, rwl, rb, *w_list)


def _moe_kernel(h_ref, c_ref, x1_ref, g2_ref, wg_ref, wu_ref, wd_ref, o_ref, acc):
    e = pl.program_id(2)

    @pl.when(e == 0)
    def _():
        acc[...] = jnp.zeros_like(acc)

    h = h_ref[...].astype(BF16)
    act = _silu(_dot(h, wg_ref[...])) * _dot(h, wu_ref[...])
    cmb = c_ref[...]
    lane = lax.broadcasted_iota(jnp.int32, cmb.shape, 1)
    ce = jnp.sum(jnp.where(lane == e, cmb, 0.0), axis=-1, keepdims=True)
    acc[...] += _dot((ce * act).astype(BF16), wd_ref[...])

    @pl.when(e == pl.num_programs(2) - 1)
    def _():
        o_ref[...] = x1_ref[...] + g2_ref[...] * acc[...]


def _moe(layer, h2, cmb, x1, mod, per_token, wg, wu, wd, tm):
    b, t, d = x1.shape
    de = wg.shape[-1]
    tmm = tm if per_token else 1
    mi = (lambda bi, ti: ti) if per_token else (lambda bi, ti: 0)
    tok = lambda w: pl.BlockSpec((None, tm, w), lambda bi, ti, e: (bi, ti, 0))
    return pl.pallas_call(
        _moe_kernel,
        out_shape=jax.ShapeDtypeStruct((b, t, d), F32),
        grid=(b, t // tm, N_EXPERTS),
        in_specs=[
            tok(d), tok(N_EXPERTS), tok(d),
            pl.BlockSpec((None, tmm, d), lambda bi, ti, e: (bi, mi(bi, ti), 5)),
            pl.BlockSpec((None, None, d, de), lambda bi, ti, e: (layer, e, 0, 0)),
            pl.BlockSpec((None, None, d, de), lambda bi, ti, e: (layer, e, 0, 0)),
            pl.BlockSpec((None, None, de, d), lambda bi, ti, e: (layer, e, 0, 0)),
        ],
        out_specs=tok(d),
        scratch_shapes=[pltpu.VMEM((tm, d), F32)],
        compiler_params=_cp(("parallel", "parallel", "arbitrary")),
        name="moe_dense",
    )(h2, cmb, x1, mod, wg, wu, wd)


def _plan_kernel(tm, cmb_ref, off_ref, pos_ref, w_ref, base):
    i = pl.program_id(0)

    @pl.when(i == 0)
    def _():
        base[...] = jnp.zeros_like(base)

    cmb = cmb_ref[...]
    sel = cmb > 0.0
    r = lax.broadcasted_iota(jnp.int32, (tm, tm), 0)
    c = lax.broadcasted_iota(jnp.int32, (tm, tm), 1)
    upto = jnp.where(r <= c, 1.0, 0.0).astype(BF16)
    incl = _dot(jnp.where(sel, 1.0, 0.0).astype(BF16), upto)
    pos = incl - 1.0 + base[...] + off_ref[...]
    base[...] = base[...] + incl[:, tm - 1:tm]
    p0 = jnp.min(jnp.where(sel, pos, 3.0e8), axis=0, keepdims=True)
    p1 = jnp.max(jnp.where(sel, pos, -1.0), axis=0, keepdims=True)
    w0 = jnp.sum(jnp.where(sel & (pos == p0), cmb, 0.0), axis=0, keepdims=True)
    w1 = jnp.sum(jnp.where(sel & (pos == p1), cmb, 0.0), axis=0, keepdims=True)
    w1 = jnp.where(p1 == p0, 0.0, w1)
    pos_ref[0:1, :] = p0.astype(jnp.int32)
    pos_ref[1:2, :] = p1.astype(jnp.int32)
    w_ref[0:1, :] = w0
    w_ref[1:2, :] = w1


def _plan(cmb_t, offsets, tm):
    b, e, t = cmb_t.shape
    nt = t // tm
    return pl.pallas_call(
        functools.partial(_plan_kernel, tm),
        out_shape=(jax.ShapeDtypeStruct((2, b * t), jnp.int32), jax.ShapeDtypeStruct((2, b * t), F32)),
        grid=(b * nt,),
        in_specs=[pl.BlockSpec((None, e, tm), lambda i: (i // nt, 0, i % nt)),
                  pl.BlockSpec((e, 1), lambda i: (0, 0))],
        out_specs=(pl.BlockSpec((2, tm), lambda i: (0, i)), pl.BlockSpec((2, tm), lambda i: (0, i))),
        scratch_shapes=[pltpu.VMEM((e, 1), F32)],
        compiler_params=_cp(("arbitrary",)),
        name="moe_plan",
    )(cmb_t, offsets)


def _dispatch_kernel(tm, pos_ref, h_ref, xs_in, xs_out, sem):
    del xs_in

    def copy(t, slot):
        return pltpu.make_async_copy(h_ref.at[pl.ds(t, 1)], xs_out.at[pl.ds(pos_ref[0, slot * tm + t], 1)], sem)

    def start(t, c):
        copy(t, 0).start()
        copy(t, 1).start()
        return c

    def wait(t, c):
        copy(t, 0).wait()
        copy(t, 1).wait()
        return c

    lax.fori_loop(0, tm, start, 0)
    lax.fori_loop(0, tm, wait, 0)


def _dispatch(h2, pos_tiles, n_rows, tm):
    b, t, d = h2.shape
    nt = t // tm
    return pl.pallas_call(
        functools.partial(_dispatch_kernel, tm),
        out_shape=jax.ShapeDtypeStruct((n_rows, d), h2.dtype),
        grid=(b, nt),
        in_specs=[pl.BlockSpec((None, 1, 2 * tm), lambda bi, ti: (bi * nt + ti, 0, 0), memory_space=pltpu.SMEM),
                  pl.BlockSpec((tm, d), lambda bi, ti: (bi * nt + ti, 0)),
                  pl.BlockSpec(memory_space=pl.ANY)],
        out_specs=pl.BlockSpec(memory_space=pl.ANY),
        scratch_shapes=[pltpu.SemaphoreType.DMA(())],
        input_output_aliases={2: 0},
        compiler_params=_cp(("arbitrary", "arbitrary")),
        name="moe_dispatch",
    )(pos_tiles, h2.reshape(b * t, d), jnp.zeros((n_rows, d), h2.dtype))


def _gmm_kernel(te_ref, nv_ref, x_ref, wg_ref, wu_ref, wd_ref, y_ref):
    i = pl.program_id(0)

    @pl.when(i < nv_ref[0])
    def _():
        x = x_ref[...].astype(BF16)
        act = _silu(_dot(x, wg_ref[...])) * _dot(x, wu_ref[...])
        y_ref[...] = _dot(act.astype(BF16), wd_ref[...])

    @pl.when(i >= nv_ref[0])
    def _():
        y_ref[...] = jnp.zeros_like(y_ref)


def _gmm(layer, xs, tile_expert, n_valid, wg, wu, wd, tr):
    n_rows, d = xs.shape
    de = wg.shape[-1]
    grid_spec = pltpu.PrefetchScalarGridSpec(
        num_scalar_prefetch=2,
        grid=(n_rows // tr,),
        in_specs=[
            pl.BlockSpec((tr, d), lambda i, te, nv: (i, 0)),
            pl.BlockSpec((None, None, d, de), lambda i, te, nv: (layer, te[i], 0, 0)),
            pl.BlockSpec((None, None, d, de), lambda i, te, nv: (layer, te[i], 0, 0)),
            pl.BlockSpec((None, None, de, d), lambda i, te, nv: (layer, te[i], 0, 0)),
        ],
        out_specs=pl.BlockSpec((tr, d), lambda i, te, nv: (i, 0)),
    )
    return pl.pallas_call(
        _gmm_kernel,
        out_shape=jax.ShapeDtypeStruct((n_rows, d), F32),
        grid_spec=grid_spec,
        compiler_params=_cp(("arbitrary",)),
        name="moe_experts",
    )(tile_expert, n_valid, xs, wg, wu, wd)


def _combine_kernel(tm, pos_ref, w_ref, x1_ref, g2_ref, ys_hbm, o_ref, buf0, buf1, sem):
    buf = (buf0, buf1)

    def copy(t, slot):
        return pltpu.make_async_copy(ys_hbm.at[pl.ds(pos_ref[0, slot * tm + t], 1)], buf[slot].at[pl.ds(t, 1)], sem)

    def start(t, c):
        copy(t, 0).start()
        copy(t, 1).start()
        return c

    def wait(t, c):
        copy(t, 0).wait()
        copy(t, 1).wait()
        return c

    lax.fori_loop(0, tm, start, 0)
    lax.fori_loop(0, tm, wait, 0)
    w = w_ref[...]
    y = w[:, 0:1] * buf0[...] + w[:, 1:2] * buf1[...]
    o_ref[...] = x1_ref[...] + g2_ref[...] * y


def _combine(ys, pos_tiles, w_tok, x1, mod, tm):
    b, t, d = x1.shape
    nt = t // tm
    return pl.pallas_call(
        functools.partial(_combine_kernel, tm),
        out_shape=jax.ShapeDtypeStruct((b, t, d), F32),
        grid=(b, nt),
        in_specs=[pl.BlockSpec((None, 1, 2 * tm), lambda bi, ti: (bi * nt + ti, 0, 0), memory_space=pltpu.SMEM),
                  pl.BlockSpec((tm, 2), lambda bi, ti: (bi * nt + ti, 0)),
                  pl.BlockSpec((None, tm, d), lambda bi, ti: (bi, ti, 0)),
                  pl.BlockSpec((None, 1, d), lambda bi, ti: (bi, 0, 5)),
                  pl.BlockSpec(memory_space=pl.ANY)],
        out_specs=pl.BlockSpec((None, tm, d), lambda bi, ti: (bi, ti, 0)),
        scratch_shapes=[pltpu.VMEM((tm, d), F32), pltpu.VMEM((tm, d), F32), pltpu.SemaphoreType.DMA(())],
        compiler_params=_cp(("arbitrary", "arbitrary")),
        name="moe_combine",
    )(pos_tiles, w_tok, x1, mod, ys)


def _moe_sorted(layer, h2, cmb_t, x1, mod, wg, wu, wd, tm, tr):
    b, t, d = x1.shape
    n = b * t
    e = cmb_t.shape[1]
    n_rows = 2 * n + e * tr
    n_tiles = n_rows // tr
    counts = jnp.sum(cmb_t > 0.0, axis=(0, 2)).astype(jnp.int32)
    padded = (counts + tr - 1) // tr * tr
    ends = jnp.cumsum(padded)
    offsets = (ends - padded).astype(F32).reshape(e, 1)
    tile_expert = jnp.minimum(jnp.sum(ends[None, :] <= (jnp.arange(n_tiles, dtype=jnp.int32) * tr)[:, None], axis=1),
                              e - 1).astype(jnp.int32)
    n_valid = (ends[-1:] // tr).astype(jnp.int32)
    pos, w = _plan(cmb_t, offsets, min(t, 512))
    pos_tiles = pos.reshape(2, n // tm, tm).transpose(1, 0, 2).reshape(n // tm, 1, 2 * tm)
    xs = _dispatch(h2, pos_tiles, n_rows, tm)
    ys = _gmm(layer, xs, tile_expert, n_valid, wg, wu, wd, tr)
    return _combine(ys, pos_tiles, w.T, x1, mod, tm)


def _rope_tables(pos):
    half = HEAD_DIM // 2
    inv = np.exp(-np.log(ROPE_BASE) * np.arange(half, dtype=np.float64) / half)
    ang = np.asarray(pos, np.float64)[:, None] * inv[None, :]
    reps = RET_WIDTH // half
    return (jnp.asarray(np.tile(np.cos(ang), (1, reps)), F32), jnp.asarray(np.tile(np.sin(ang), (1, reps)), F32))


def _block_diag(w):
    depth, nb, h, _ = w.shape
    eye = jnp.eye(nb, dtype=w.dtype)
    return (w[:, :, :, None, :] * eye[None, :, None, :, None]).reshape(depth, nb * h, nb * h)


def _group_mean_matrix(width):
    idx = np.arange(width) // HEAD_DIM
    return jnp.asarray((idx[:, None] == idx[None, :]) / HEAD_DIM, BF16)


def _trunk(x, mod, per_token, seq_shape, pos, lru_h0, lru_buf0, ret_r0, ret_chunk, attend, prm, tm, tm_moe, precise,
           log2_scores):
    bv, tv, d = x.shape
    b, t = seq_shape
    depth = prm['w_in_hl'][0].shape[0]
    cos_t, sin_t = _rope_tables(pos)
    kv = None
    logfs, lru_hs, lru_bufs, rets = [], [], [], []
    npair = RET_HEADS // 2
    for l in range(depth):
        m = mod[l]
        xbgb, q16, k32, k16, v32, v16, rqkv, rg, lf = _proj(
            l, x, m, per_token, prm['norm1_w'], prm['w_in_hl'], prm['q_norm_w'], prm['k_norm_w'], prm['forget_bias'],
            prm['g512'], cos_t, sin_t, kv, tm, precise or l < depth - 1, F32 if precise else BF16,
            HEAD_DIM ** -0.5 * (LOG2E if log2_scores else 1.0))
        kv = (k32, v32)
        seq = lambda a: a.reshape(b, t, a.shape[-1])
        pick = lambda hl: tuple(w[l] for w in hl)
        ya, lru_h, lru_buf = _lru(seq(xbgb), lru_h0[l], lru_buf0[l], prm['conv_w'][l], prm['conv_b'][l],
                                  pick(prm['lru_wa_hl']), prm['lru_ba'][l], pick(prm['lru_wx_hl']), prm['lru_bx'][l],
                                  prm['lru_lambda'][l], precise, F32)
        lf_s = seq(lf)
        cum, cumt = _cum(lf_s, LOG2E if log2_scores else 1.0)
        yb = attend(l, seq(q16), seq(k16), seq(v16), cum, cumt)
        yc, r_out = _retention(seq(rqkv), seq(rg), ret_r0[l], prm['ret_norm_w'][l], prm['g128'], ret_chunk, precise, F32)
        tokv = lambda a: a.reshape(bv, tv, a.shape[-1])
        x1, h2, cmb_t = _out_proj(l, tokv(ya), tokv(yb), tokv(yc), x, m, per_token, prm['w_out_hl'], prm['norm2_w'],
                                  prm['rw_hi'], prm['rw_lo'], prm['router_bias'], tm, True)
        if per_token:
            x = _moe(l, h2, cmb_t.transpose(0, 2, 1), x1, m, per_token, prm['w_gate16'], prm['w_up16'],
                     prm['w_down16'], tm_moe)
        else:
            x = _moe_sorted(l, h2, cmb_t, x1, m, prm['w_gate16'], prm['w_up16'], prm['w_down16'], min(t, 256), tm_moe)
        logfs.append(lf_s[:, :, :FOX_HEADS])
        lru_hs.append(lru_h[:, 0, :])
        lru_bufs.append(lru_buf)
        r4 = r_out.reshape(b, npair, 2, HEAD_DIM, 2, HEAD_DIM)
        rets.append(jnp.stack([r4[:, :, 0, :, 0, :], r4[:, :, 1, :, 1, :]], axis=2).reshape(
            b, RET_HEADS, HEAD_DIM, HEAD_DIM))
    k_out = kv[0].reshape(depth, b, t, FOX_HEADS, HEAD_DIM)
    v_out = kv[1].reshape(depth, b, t, FOX_HEADS, HEAD_DIM)
    return x, (k_out, v_out, jnp.stack(logfs), jnp.stack(lru_hs), jnp.stack(lru_bufs), jnp.stack(rets))


def _pair_state(r):
    depth, b, h, d, _ = r.shape
    r6 = r.reshape(depth, b, h // 2, 2, d, d)
    eye = jnp.eye(2, dtype=r.dtype)
    out = r6[:, :, :, :, :, None, :] * eye[None, None, None, :, None, :, None]
    return out.reshape(depth, b, h // 2, 2 * d, 2 * d)


def kernel(x_prompt, x_sample, cache_k, cache_v, cache_logf, state_lru_h, state_lru_conv, state_ret, page_table,
           c_prompt, c_sample, w_in, w_out, conv_w, conv_b, lru_wa, lru_ba, lru_wx, lru_bx, lru_lambda, q_norm_w,
           k_norm_w, forget_bias, ret_norm_w, norm1_w, norm2_w, w_ada, b_ada, router_w, router_bias, w_gate, w_up,
           w_down):
    depth, d, _ = w_in.shape
    bp = x_prompt.shape[0]
    bs, ts, _ = x_sample.shape
    prm = _prep(w_in=w_in, w_out=w_out, conv_w=conv_w, conv_b=conv_b, lru_wa=lru_wa, lru_ba=lru_ba, lru_wx=lru_wx,
                lru_bx=lru_bx, lru_lambda=lru_lambda, q_norm_w=q_norm_w, k_norm_w=k_norm_w, forget_bias=forget_bias,
                ret_norm_w=ret_norm_w, norm1_w=norm1_w, norm2_w=norm2_w, router_w=router_w, router_bias=router_bias,
                w_gate=w_gate, w_up=w_up, w_down=w_down)

    mod = _ada(jnp.concatenate([c_prompt, c_sample], axis=0), w_ada, b_ada)
    mod_p = mod[:, :bp].reshape(depth, bp, 1, 6 * d)
    mod_s = jnp.repeat(mod[:, bp:], ts, axis=1).reshape(depth, 1, bs * ts, 6 * d)
    return _run(prm, mod_p, mod_s, x_prompt, x_sample, cache_k, cache_v, cache_logf, state_lru_h, state_lru_conv,
                state_ret, page_table)


def _prep(w_in, w_out, conv_w, conv_b, lru_wa, lru_ba, lru_wx, lru_bx, lru_lambda, q_norm_w, k_norm_w, forget_bias,
          ret_norm_w, norm1_w, norm2_w, router_w, router_bias, w_gate, w_up, w_down, **unused):
    depth, d, n_in = w_in.shape
    ff0 = 2 * LRU_WIDTH + 3 * FOX_WIDTH
    w_in_p = jnp.concatenate(
        [w_in[:, :, :ff0], w_in[:, :, ff0 + FOX_HEADS:], w_in[:, :, ff0:ff0 + FOX_HEADS],
         jnp.zeros((depth, d, N_PROJ - n_in), w_in.dtype)], axis=-1)
    rw_hi, rw_lo = _split2(router_w.T)
    return dict(
        w_in_hl=_split2(w_in_p),
        norm1_w=norm1_w.reshape(depth, 1, d),
        norm2_w=norm2_w.reshape(depth, 1, d),
        q_norm_w=jnp.tile(q_norm_w, (1, FOX_HEADS)).reshape(depth, 1, FOX_WIDTH),
        k_norm_w=jnp.tile(k_norm_w, (1, FOX_HEADS)).reshape(depth, 1, FOX_WIDTH),
        forget_bias=jnp.pad(forget_bias, ((0, 0), (0, LANES - FOX_HEADS))).reshape(depth, 1, LANES),
        g512=_group_mean_matrix(FOX_WIDTH),
        g128=_group_mean_matrix(LANES),
        conv_w=conv_w,
        conv_b=conv_b.reshape(depth, 1, LRU_WIDTH),
        lru_wa_hl=_split2(_block_diag(lru_wa)),
        lru_wx_hl=_split2(_block_diag(lru_wx)),
        lru_ba=lru_ba.reshape(depth, 1, LRU_WIDTH),
        lru_bx=lru_bx.reshape(depth, 1, LRU_WIDTH),
        lru_lambda=lru_lambda.reshape(depth, 1, LRU_WIDTH),
        ret_norm_w=ret_norm_w.reshape(depth, 1, RET_WIDTH),
        w_out_hl=_split2(w_out),
        rw_hi=rw_hi, rw_lo=rw_lo,
        router_bias=router_bias.reshape(N_EXPERTS, 1),
        w_gate16=w_gate.astype(BF16), w_up16=w_up.astype(BF16), w_down16=w_down.astype(BF16),
    )


def _run(prm, mod_p, mod_s, x_prompt, x_sample, cache_k, cache_v, cache_logf, state_lru_h, state_lru_conv, state_ret,
         page_table):
    depth = prm['w_in_hl'][0].shape[0]
    bp, tp, d = x_prompt.shape
    bs, ts, _ = x_sample.shape
    n_pages = page_table.shape[1]
    past = n_pages * cache_k.shape[2]

    npair = RET_HEADS // 2
    zeros = lambda *s: jnp.zeros(s, F32)
    tq = min(tp, 512)
    y_prompt, outs_p = _trunk(
        x_prompt, mod_p, False, (bp, tp), np.arange(tp),
        zeros(depth, bp, 1, LRU_WIDTH), zeros(depth, bp, CONV_W - 1, LRU_WIDTH),
        zeros(depth, bp, npair, LANES, LANES), min(tp, 128),
        lambda l, q, k, v, cum, cumt: _fox_prompt(q, k, v, cumt, tq, F32),
        prm, min(tp, 512), min(tp, 512), False, True)

    pp = 8 if n_pages % 8 == 0 else 1

    def attend_sample(l, q, k, v, cum, cumt):
        bias = _suffix_bias(l, cache_logf, page_table)
        return _fox_decode(l, q, k, v, cum, cache_k, cache_v, bias, page_table, pp, F32)

    n_s = bs * ts
    y_s, outs_s = _trunk(
        x_sample.reshape(1, n_s, d), mod_s, True, (bs, ts), past + np.tile(np.arange(ts), bs),
        state_lru_h.reshape(depth, bs, 1, LRU_WIDTH), state_lru_conv, _pair_state(state_ret), ts,
        attend_sample, prm, n_s, n_s, True, False)
    return (y_prompt, y_s.reshape(bs, ts, d)) + tuple(outs_p) + tuple(outs_s)
```

```python
import functools
import math

import jax
import jax.numpy as jnp
import numpy as np
from jax import lax
from jax.experimental import pallas as pl
from jax.experimental.pallas import tpu as pltpu

F32 = jnp.float32
BF16 = jnp.bfloat16

HEAD_DIM = 64
LRU_WIDTH = 256
LRU_BLOCKS = 4
CONV_W = 4
LRU_C = 8.0
FOX_WIDTH = 512
FOX_HEADS = 8
RET_WIDTH = 256
RET_HEADS = 4
ROPE_BASE = 10000.0
N_EXPERTS = 16
N_GROUPS = 4
EXPERTS_PER_GROUP = 4
EPS = 1e-6
LANES = 128
LOG2E = math.log2(math.e)
N_PROJ = 3200

VMEM_LIMIT = 56 * 1024 * 1024


def _cp(sem):
    return pltpu.CompilerParams(dimension_semantics=sem, vmem_limit_bytes=VMEM_LIMIT)


def _dot(a, b):
    return jnp.dot(a, b, preferred_element_type=F32)


def _dot_nt(a, b):
    return lax.dot_general(a, b, (((1,), (1,)), ((), ())), preferred_element_type=F32)


def _round_bf16(x):
    bits = lax.bitcast_convert_type(x, jnp.uint32)
    bits = (bits + jnp.uint32(0x7FFF) + ((bits >> 16) & jnp.uint32(1))) & jnp.uint32(0xFFFF0000)
    return lax.bitcast_convert_type(bits, F32)


def _split2(x):
    hi = _round_bf16(x)
    return hi.astype(BF16), (x - hi).astype(BF16)


def _split3(x):
    hi = _round_bf16(x)
    r = x - hi
    mid = _round_bf16(r)
    return hi.astype(BF16), mid.astype(BF16), (r - mid).astype(BF16)


def _dot_f32(a, b):
    ah, al = _split2(a)
    bh, bl = _split2(b)
    return _dot(ah, bh) + (_dot(ah, bl) + _dot(al, bh))


def _dot_exact_rhs(a, m):
    hi, mid, lo = _split3(a)
    return _dot(hi, m) + (_dot(mid, m) + _dot(lo, m))


def _lhs(a, precise):
    return _split2(a.astype(F32)) if precise else (a.astype(BF16),)


def _mm(lhs, w_refs, cols=None):
    sl = (lambda r: r[...]) if cols is None else (lambda r: r[cols[0], cols[1]])
    if len(w_refs) == 1:
        return _dot(lhs[0], sl(w_refs[0]))
    wh, wl = sl(w_refs[0]), sl(w_refs[1])
    return _dot(lhs[0], wh) + (_dot(lhs[0], wl) + _dot(lhs[1], wh))


def _dx(a, b, precise, nt=False):
    f = _dot_nt if nt else _dot
    if not precise:
        return f(a.astype(BF16), b.astype(BF16))
    ah, al = _split2(a.astype(F32))
    bh, bl = _split2(b.astype(F32))
    return f(ah, bh) + (f(ah, bl) + f(al, bh))


def _idiv(x, n):
    assert n & (n - 1) == 0
    return lax.shift_right_logical(x, n.bit_length() - 1)


def _sigmoid(x):
    return 1.0 / (1.0 + jnp.exp(-x))


def _silu(x):
    return x * _sigmoid(x)


def _softplus(x):
    return jnp.maximum(x, 0.0) + jnp.log1p(jnp.exp(-jnp.abs(x)))


def _gelu_tanh(x):
    c = math.sqrt(2.0 / math.pi)
    return 0.5 * x * (1.0 + jnp.tanh(c * (x + 0.044715 * (x * x * x))))


def _rms(x, w):
    ms = jnp.mean(x * x, axis=-1, keepdims=True)
    return x * lax.rsqrt(ms + EPS) * w


def _ada_kernel(c_ref, w_ref, b_ref, o_ref):
    s = _silu(c_ref[...])
    o_ref[...] = _dot_f32(s, w_ref[...]) + b_ref[...]


def _ada(c_all, w_ada, b_ada):
    depth, d, n6 = w_ada.shape
    bc = c_all.shape[0]
    tn = 1024
    return pl.pallas_call(
        _ada_kernel,
        out_shape=jax.ShapeDtypeStruct((depth, bc, n6), F32),
        grid=(depth, n6 // tn),
        in_specs=[
            pl.BlockSpec((bc, d), lambda l, j: (0, 0)),
            pl.BlockSpec((None, d, tn), lambda l, j: (l, 0, j)),
            pl.BlockSpec((None, 1, tn), lambda l, j: (l, 0, j)),
        ],
        out_specs=pl.BlockSpec((None, bc, tn), lambda l, j: (l, 0, j)),
        compiler_params=_cp(("parallel", "parallel")),
        name="ada_mod",
    )(c_all, w_ada, b_ada.reshape(depth, 1, n6))


def _proj_kernel(has_alias, precise, q_scale, x_ref, sh_ref, sc_ref, n1_ref, qw_ref, kw_ref, fb_ref, g_ref,
                 cos_ref, sin_ref, *rest):
    nw = 2 if precise else 1
    w_refs, rest = rest[:nw], rest[nw:]
    if has_alias:
        rest = rest[2:]
    xbgb_ref, q_ref, k32_ref, k16_ref, v32_ref, v16_ref, rqkv_ref, rg_ref, lf_ref = rest
    x = x_ref[...]
    h = _rms(x, n1_ref[...]) * (1.0 + sc_ref[...]) + sh_ref[...]
    hl = _lhs(h, precise)
    g = g_ref[...]

    def seg(lo, hi):
        return _mm(hl, w_refs, (slice(None), slice(lo, hi)))

    xbgb_ref[...] = seg(0, 512)

    def head_norm(t, w):
        hi, lo = _split2(t * t)
        ms = _dot(hi, g) + _dot(lo, g)
        return t * lax.rsqrt(ms + EPS) * w

    q = head_norm(seg(512, 1024), qw_ref[...])
    q_ref[...] = (q * q_scale).astype(q_ref.dtype)

    def store_layer(ref, val):
        if has_alias:
            ref[...] = val
        else:
            ref[0] = val
            ref[1:] = jnp.zeros((ref.shape[0] - 1,) + val.shape, val.dtype)

    k = head_norm(seg(1024, 1536), kw_ref[...])
    store_layer(k32_ref, k)
    k16_ref[...] = k.astype(BF16)
    v = seg(1536, 2048)
    store_layer(v32_ref, v)
    v16_ref[...] = v.astype(BF16)

    cos = cos_ref[...]
    sin = sin_ref[...]
    first_half = (lax.broadcasted_iota(jnp.int32, cos.shape, 1) & 32) == 0

    def rope(t):
        rot = jnp.where(first_half, -pltpu.roll(t, RET_WIDTH - 32, 1), pltpu.roll(t, 32, 1))
        return t * cos + rot * sin

    rdt = rqkv_ref.dtype
    rqkv_ref[:, 0:256] = rope(seg(2048, 2304)).astype(rdt)
    rqkv_ref[:, 256:512] = (rope(seg(2304, 2560)) * (HEAD_DIM ** -0.5)).astype(rdt)
    rqkv_ref[:, 512:768] = seg(2560, 2816).astype(rdt)
    rg_ref[...] = seg(2816, 3072)

    ff = seg(3072, 3200) + fb_ref[...]
    lf_ref[...] = -_softplus(-ff)


def _proj(layer, x, mod, per_token, n1, w_in_hl, qw, kw, fb, gmat, cos_t, sin_t, kv_prev, tm, precise, adt, q_scale):
    b, t, d = x.shape
    depth = w_in_hl[0].shape[0]
    nt = t // tm
    tmm = tm if per_token else 1
    mi = (lambda bi, ti: ti) if per_token else (lambda bi, ti: 0)
    has_alias = kv_prev is not None
    w_list = list(w_in_hl) if precise else [w_in_hl[0]]
    in_specs = [
        pl.BlockSpec((None, tm, d), lambda bi, ti: (bi, ti, 0)),
        pl.BlockSpec((None, tmm, d), lambda bi, ti: (bi, mi(bi, ti), 0)),
        pl.BlockSpec((None, tmm, d), lambda bi, ti: (bi, mi(bi, ti), 1)),
        pl.BlockSpec((None, 1, d), lambda bi, ti: (layer, 0, 0)),
        pl.BlockSpec((None, 1, FOX_WIDTH), lambda bi, ti: (layer, 0, 0)),
        pl.BlockSpec((None, 1, FOX_WIDTH), lambda bi, ti: (layer, 0, 0)),
        pl.BlockSpec((None, 1, LANES), lambda bi, ti: (layer, 0, 0)),
        pl.BlockSpec((FOX_WIDTH, FOX_WIDTH), lambda bi, ti: (0, 0)),
        pl.BlockSpec((tm, RET_WIDTH), lambda bi, ti: (ti, 0)),
        pl.BlockSpec((tm, RET_WIDTH), lambda bi, ti: (ti, 0)),
    ] + [pl.BlockSpec((None, d, N_PROJ), lambda bi, ti: (layer, 0, 0), pipeline_mode=pl.Buffered(1)) for _ in w_list]
    args = [x, mod, mod, n1, qw, kw, fb, gmat, cos_t, sin_t] + w_list
    aliases = {}
    if has_alias:
        in_specs += [pl.BlockSpec(memory_space=pl.ANY), pl.BlockSpec(memory_space=pl.ANY)]
        aliases = {len(args): 2, len(args) + 1: 4}
        args += list(kv_prev)
    tok = lambda w, dt: jax.ShapeDtypeStruct((b, t, w), dt)
    kv_shape = jax.ShapeDtypeStruct((depth, b, t, FOX_WIDTH), F32)
    tok_spec = lambda w: pl.BlockSpec((None, tm, w), lambda bi, ti: (bi, ti, 0))
    if has_alias:
        kv_spec = pl.BlockSpec((None, None, tm, FOX_WIDTH), lambda bi, ti: (layer, bi, ti, 0))
    else:
        assert layer == 0
        kv_spec = pl.BlockSpec((depth, None, tm, FOX_WIDTH), lambda bi, ti: (0, bi, ti, 0))
    return pl.pallas_call(
        functools.partial(_proj_kernel, has_alias, precise, q_scale),
        out_shape=(tok(512, F32), tok(FOX_WIDTH, adt), kv_shape, tok(FOX_WIDTH, BF16), kv_shape,
                   tok(FOX_WIDTH, BF16), tok(768, adt), tok(RET_WIDTH, F32), tok(LANES, F32)),
        grid=(b, nt),
        in_specs=in_specs,
        out_specs=(tok_spec(512), tok_spec(FOX_WIDTH), kv_spec, tok_spec(FOX_WIDTH), kv_spec,
                   tok_spec(FOX_WIDTH), tok_spec(768), tok_spec(RET_WIDTH), tok_spec(LANES)),
        input_output_aliases=aliases,
        compiler_params=_cp(("parallel", "parallel")),
        name="norm_proj",
    )(*args)


def _lru_kernel(ch, precise, x_ref, h0_ref, buf_ref, cw_ref, cb_ref, ba_ref, bx_ref, lam_ref, *rest):
    nw = 2 if precise else 1
    wa_refs, wx_refs, rest = rest[:nw], rest[nw:2 * nw], rest[2 * nw:]
    y_ref, hl_ref, bo_ref, xpad, a_s, b_s, h_s, hc = rest
    ci = pl.program_id(1)

    @pl.when(ci == 0)
    def _():
        xpad[0:8, :] = jnp.zeros((8, LRU_WIDTH), F32)
        xpad[5:8, :] = buf_ref[...]
        hc[...] = jnp.broadcast_to(h0_ref[...], (8, LRU_WIDTH))

    xb = x_ref[:, 0:LRU_WIDTH]
    gb = x_ref[:, LRU_WIDTH:2 * LRU_WIDTH]
    xpad[8:8 + ch, :] = xb
    xc = cw_ref[0:1, :] * xpad[5:5 + ch, :]
    xc = xc + cw_ref[1:2, :] * xpad[6:6 + ch, :]
    xc = xc + cw_ref[2:3, :] * xpad[7:7 + ch, :]
    xc = xc + cw_ref[3:4, :] * xb
    xc = xc + cb_ref[...]
    xl = _lhs(xc, precise)
    r = _sigmoid(_mm(xl, wa_refs) + ba_ref[...])
    i = _sigmoid(_mm(xl, wx_refs) + bx_ref[...])
    log_a = (-LRU_C) * r * _softplus(-lam_ref[...])
    a_s[...] = jnp.exp(log_a)
    th = jnp.tanh(log_a)
    b_s[...] = jnp.sqrt(-2.0 * th / (1.0 - th)) * i * xc

    row = lax.broadcasted_iota(jnp.int32, (8, LRU_WIDTH), 0)

    def blk(j, h):
        r0 = pl.multiple_of(j * 8, 8)
        a = a_s[pl.ds(r0, 8), :]
        bv = b_s[pl.ds(r0, 8), :]
        for s in (1, 2, 4):
            keep = row >= s
            a_sh = jnp.where(keep, pltpu.roll(a, s, 0), 1.0)
            b_sh = jnp.where(keep, pltpu.roll(bv, s, 0), 0.0)
            bv = a * b_sh + bv
            a = a * a_sh
        hb = a * h + bv
        h_s[pl.ds(r0, 8), :] = hb
        return jnp.broadcast_to(hb[7:8, :], (8, LRU_WIDTH))

    h_end = lax.fori_loop(0, ch // 8, blk, hc[...])
    hc[...] = h_end
    y_ref[...] = (h_s[...] * _gelu_tanh(gb)).astype(y_ref.dtype)
    tail = xpad[ch:ch + 8, :]
    xpad[0:8, :] = tail

    @pl.when(ci == pl.num_programs(1) - 1)
    def _():
        hl_ref[...] = h_end[0:1, :]
        bo_ref[...] = tail[5:8, :]


def _lru(xbgb, h0, buf, cw, cb, wa_hl, ba, wx_hl, bx, lam, precise, ydt):
    b, t, _ = xbgb.shape
    ch = min(t, 256)
    w = LRU_WIDTH
    vec = lambda: pl.BlockSpec((1, w), lambda bi, ci: (0, 0))
    mats = (list(wa_hl) + list(wx_hl)) if precise else [wa_hl[0], wx_hl[0]]
    return pl.pallas_call(
        functools.partial(_lru_kernel, ch, precise),
        out_shape=(jax.ShapeDtypeStruct((b, t, w), ydt), jax.ShapeDtypeStruct((b, 1, w), F32),
                   jax.ShapeDtypeStruct((b, CONV_W - 1, w), F32)),
        grid=(b, t // ch),
        in_specs=[
            pl.BlockSpec((None, ch, 2 * w), lambda bi, ci: (bi, ci, 0)),
            pl.BlockSpec((None, 1, w), lambda bi, ci: (bi, 0, 0)),
            pl.BlockSpec((None, CONV_W - 1, w), lambda bi, ci: (bi, 0, 0)),
            pl.BlockSpec((CONV_W, w), lambda bi, ci: (0, 0)),
            vec(), vec(), vec(), vec(),
        ] + [pl.BlockSpec((w, w), lambda bi, ci: (0, 0)) for _ in mats],
        out_specs=(pl.BlockSpec((None, ch, w), lambda bi, ci: (bi, ci, 0)),
                   pl.BlockSpec((None, 1, w), lambda bi, ci: (bi, 0, 0)),
                   pl.BlockSpec((None, CONV_W - 1, w), lambda bi, ci: (bi, 0, 0))),
        scratch_shapes=[pltpu.VMEM((ch + 8, w), F32), pltpu.VMEM((ch, w), F32), pltpu.VMEM((ch, w), F32),
                        pltpu.VMEM((ch, w), F32), pltpu.VMEM((8, w), F32)],
        compiler_params=_cp(("parallel", "arbitrary")),
        name="rglru",
    )(xbgb, h0, buf, cw, cb, ba, bx, lam, *mats)


def _cum_kernel(t, tc, tscale, lf_ref, cum_ref, cumt_ref, carry):
    ci = pl.program_id(1)

    @pl.when(ci == 0)
    def _():
        carry[...] = jnp.zeros_like(carry)

    lf = lf_ref[...]
    if t < tc:
        lf = jnp.concatenate([lf, jnp.zeros((tc - t, LANES), F32)], axis=0)
    r = lax.broadcasted_iota(jnp.int32, (tc, tc), 0)
    c = lax.broadcasted_iota(jnp.int32, (tc, tc), 1)
    tri = jnp.where(c <= r, 1.0, 0.0).astype(BF16)
    hi, mid, lo = _split3(lf)
    cum = _dot(tri, hi) + (_dot(tri, mid) + _dot(tri, lo)) + carry[0:1, :]
    carry[...] = jnp.broadcast_to(cum[tc - 1:tc, :], carry.shape)
    cum_ref[...] = cum[0:t, :] if t < tc else cum
    cumt_ref[...] = cum.T[0:FOX_HEADS, :] * tscale


def _cum(lf, tscale):
    b, t, _ = lf.shape
    tc = 128 if t < 128 else min(t, 512)
    tb = min(t, tc)
    nt = max(t // tc, 1)
    return pl.pallas_call(
        functools.partial(_cum_kernel, tb, tc, tscale),
        out_shape=(jax.ShapeDtypeStruct((b, t, LANES), F32), jax.ShapeDtypeStruct((b, FOX_HEADS, nt * tc), F32)),
        grid=(b, nt),
        in_specs=[pl.BlockSpec((None, tb, LANES), lambda bi, ci: (bi, ci, 0))],
        out_specs=(pl.BlockSpec((None, tb, LANES), lambda bi, ci: (bi, ci, 0)),
                   pl.BlockSpec((None, FOX_HEADS, tc), lambda bi, ci: (bi, 0, ci))),
        scratch_shapes=[pltpu.VMEM((8, LANES), F32)],
        compiler_params=_cp(("parallel", "arbitrary")),
        name="logf_cumsum",
    )(lf)


def _fox_kernel(tq, ts, q_ref, k_ref, v_ref, cumt_ref, o_ref, m_s, l_s, acc_s):
    hp = pl.program_id(1)
    qi = pl.program_id(2)
    n_strips = tq // ts
    q = q_ref[...]
    lane = lax.broadcasted_iota(jnp.int32, (tq, LANES), 1)
    low = lane < HEAD_DIM
    zero = jnp.zeros_like(q)
    qm = (jnp.where(low, q, zero), jnp.where(low, zero, q))
    m_s[...] = jnp.full_like(m_s, -jnp.inf)
    l_s[...] = jnp.zeros_like(l_s)
    acc_s[...] = jnp.zeros_like(acc_s)

    def strip(hh, r, k, v, ck, diagonal, state):
        ncol = (r + 1) * ts if diagonal else tq
        s = _dot_nt(qm[hh][r * ts:(r + 1) * ts, :], k[0:ncol, :]) - ck[:, 0:ncol]
        if diagonal:
            rr = lax.broadcasted_iota(jnp.int32, (ts, ncol), 0)
            cc = lax.broadcasted_iota(jnp.int32, (ts, ncol), 1)
            s = jnp.where(cc <= rr + r * ts, s, -jnp.inf)
        m_prev, l_prev, acc_prev = state
        m_new = jnp.maximum(m_prev, jnp.max(s, axis=-1, keepdims=True))
        alpha = jnp.exp2(m_prev - m_new)
        p = jnp.exp2(s - jnp.concatenate([m_new] * (ncol // LANES), axis=1))
        l_new = alpha * l_prev + jnp.sum(p, axis=-1, keepdims=True)
        acc_new = alpha * acc_prev + _dot(p.astype(BF16), v[0:ncol, :])
        return m_new, l_new, acc_new

    def tile(ki, diagonal):
        k0 = pl.multiple_of(ki * tq, tq)
        k = k_ref[pl.ds(k0, tq), :]
        v = v_ref[pl.ds(k0, tq), :]
        chains = [(hh, r) for hh in range(2) for r in range(n_strips)]
        rows = lambda r: slice(r * ts, (r + 1) * ts)
        states = [(m_s[hh, rows(r), :], l_s[hh, rows(r), :], acc_s[hh, rows(r), :]) for hh, r in chains]
        cks = [cumt_ref[pl.ds(2 * hp + hh, 1), pl.ds(k0, tq)] for hh in range(2)]
        new = [strip(hh, r, k, v, cks[hh], diagonal, st) for (hh, r), st in zip(chains, states)]
        for (hh, r), (m_new, l_new, acc_new) in zip(chains, new):
            m_s[hh, rows(r), :] = m_new
            l_s[hh, rows(r), :] = l_new
            acc_s[hh, rows(r), :] = acc_new

    def body(ki, c):
        tile(ki, False)
        return c

    lax.fori_loop(0, qi, body, 0)
    tile(qi, True)
    o_ref[...] = jnp.where(low, acc_s[0] / l_s[0], acc_s[1] / l_s[1]).astype(o_ref.dtype)


def _fox_prompt(q16, k16, v16, cumt2, tq, ydt):
    b, t, _ = q16.shape
    npair = FOX_HEADS // 2
    ts = min(tq, LANES)
    return pl.pallas_call(
        functools.partial(_fox_kernel, tq, ts),
        out_shape=jax.ShapeDtypeStruct((b, t, FOX_WIDTH), ydt),
        grid=(b, npair, t // tq),
        in_specs=[
            pl.BlockSpec((None, tq, LANES), lambda bi, hp, qi: (bi, qi, hp)),
            pl.BlockSpec((None, t, LANES), lambda bi, hp, qi: (bi, 0, hp)),
            pl.BlockSpec((None, t, LANES), lambda bi, hp, qi: (bi, 0, hp)),
            pl.BlockSpec((None, FOX_HEADS, t), lambda bi, hp, qi: (bi, 0, 0)),
        ],
        out_specs=pl.BlockSpec((None, tq, LANES), lambda bi, hp, qi: (bi, qi, hp)),
        scratch_shapes=[pltpu.VMEM((2, tq, LANES), F32), pltpu.VMEM((2, tq, LANES), F32),
                        pltpu.VMEM((2, tq, LANES), F32)],
        compiler_params=_cp(("parallel", "parallel", "arbitrary")),
        name="fox_prompt",
    )(q16, k16, v16, cumt2)


def _suffix_kernel(layer, n_pages, nh, ps, pt_ref, lf_hbm, later_ref, o_ref, xbuf, sem):
    bi = pl.program_id(0)

    def copy(p):
        return pltpu.make_async_copy(lf_hbm.at[layer, pt_ref[bi, p]], xbuf.at[p], sem)

    def start(p, c):
        copy(p).start()
        return c

    def wait(p, c):
        copy(p).wait()
        return c

    lax.fori_loop(0, n_pages, start, 0)
    lax.fori_loop(0, n_pages, wait, 0)
    x = xbuf[...].reshape(n_pages * nh, ps)
    r = lax.broadcasted_iota(jnp.int32, (ps, ps), 0)
    c = lax.broadcasted_iota(jnp.int32, (ps, ps), 1)
    after = jnp.where(r > c, 1.0, 0.0).astype(BF16)
    within = _dot_exact_rhs(x, after)
    tot = _dot_exact_rhs(x, jnp.ones((ps, ps), BF16))
    hi, mid, lo = _split3(tot)
    later = later_ref[...]
    carry = _dot(later, hi) + (_dot(later, mid) + _dot(later, lo))
    o_ref[...] = (within + carry).reshape(n_pages, nh, ps)


def _suffix_bias(layer, cache_logf, page_table):
    depth, n_pool, ps, nh = cache_logf.shape
    b, n_pages = page_table.shape
    lf = cache_logf.transpose(0, 1, 3, 2)
    idx = np.arange(n_pages * nh)
    pg, head = idx // nh, idx % nh
    later = jnp.asarray((head[:, None] == head[None, :]) & (pg[None, :] > pg[:, None]), BF16)
    grid_spec = pltpu.PrefetchScalarGridSpec(
        num_scalar_prefetch=1,
        grid=(b,),
        in_specs=[
            pl.BlockSpec(memory_space=pl.ANY),
            pl.BlockSpec((n_pages * nh, n_pages * nh), lambda bi, pt: (0, 0)),
        ],
        out_specs=pl.BlockSpec((None, n_pages, nh, ps), lambda bi, pt: (bi, 0, 0, 0)),
        scratch_shapes=[pltpu.VMEM((n_pages, nh, ps), F32), pltpu.SemaphoreType.DMA(())],
    )
    return pl.pallas_call(
        functools.partial(_suffix_kernel, layer, n_pages, nh, ps),
        out_shape=jax.ShapeDtypeStruct((b, n_pages, nh, ps), F32),
        grid_spec=grid_spec,
        compiler_params=_cp(("arbitrary",)),
        name="logf_suffix",
    )(page_table, lf, later)


def _decode_kernel(pp, nh, ps, s_new, pt_ref, q_ref, cb_ref, bias_ref, kn_ref, vn_ref, nb_ref, *rest):
    k_refs = rest[:pp]
    v_refs = rest[pp:2 * pp]
    o_ref, m_s, l_s, acc_s = rest[2 * pp:]
    j = pl.program_id(1)
    rows = nh * s_new
    w = nh * HEAD_DIM

    @pl.when(j == 0)
    def _():
        m_s[...] = jnp.full_like(m_s, -jnp.inf)
        l_s[...] = jnp.zeros_like(l_s)
        acc_s[...] = jnp.zeros_like(acc_s)

    q = q_ref[...]
    cb = cb_ref[...]

    def qk(kt):
        s2 = _dot(q, kt)
        return s2[0:rows, :] + s2[rows:2 * rows, :]

    def step(s_list, vt_list):
        m = m_s[...]
        m_new = m
        for s in s_list:
            m_new = jnp.maximum(m_new, jnp.max(s, axis=-1, keepdims=True))
        alpha = jnp.exp(m - m_new)
        l = alpha * l_s[...]
        acc = alpha * acc_s[...]
        for s, vt in zip(s_list, vt_list):
            p = jnp.exp(s - m_new)
            l = l + jnp.sum(p, axis=-1, keepdims=True)
            acc = acc + _dot_nt(p.astype(BF16), vt)
        m_s[...] = m_new
        l_s[...] = l
        acc_s[...] = acc

    s_list, vt_list = [], []
    for r in range(pp):
        kt = k_refs[r][...].reshape(w, ps).astype(BF16)
        bias = bias_ref[r]
        bias = jnp.concatenate([jnp.broadcast_to(bias[h:h + 1, :], (s_new, ps)) for h in range(nh)], axis=0)
        s_list.append(qk(kt) + (bias + cb))
        vt_list.append(v_refs[r][...].reshape(w, ps).astype(BF16))
    step(s_list, vt_list)

    @pl.when(j == pl.num_programs(1) - 1)
    def _():
        rr = lax.broadcasted_iota(jnp.int32, (rows, ps), 0)
        cc = lax.broadcasted_iota(jnp.int32, (rows, ps), 1)
        ok = cc <= (rr & (s_new - 1))
        s = qk(kn_ref[...]) + (cb - nb_ref[...])
        step([jnp.where(ok, s, -jnp.inf)], [vn_ref[...]])
        o = acc_s[...] / l_s[...]
        rh = _idiv(lax.broadcasted_iota(jnp.int32, (rows, w), 0), s_new)
        ch = _idiv(lax.broadcasted_iota(jnp.int32, (rows, w), 1), HEAD_DIM)
        o = jnp.where(rh == ch, o, 0.0)
        out = o[0:s_new, :]
        for h in range(1, nh):
            out = out + o[h * s_new:(h + 1) * s_new, :]
        o_ref[...] = out.astype(o_ref.dtype)


def _fox_decode(layer, q32, k16, v16, cum, cache_k, cache_v, bias, page_table, pp, ydt):
    depth, n_pool, ps, nh, hd = cache_k.shape
    b, s_new, w = q32.shape
    n_pages = page_table.shape[1]
    rows = nh * s_new
    assert s_new & (s_new - 1) == 0 and nh & (nh - 1) == 0 and s_new <= ps
    ckt = cache_k.transpose(0, 1, 3, 4, 2)
    cvt = cache_v.transpose(0, 1, 3, 4, 2)
    q4 = q32.reshape(b, s_new, nh, hd)
    eye = jnp.eye(nh, dtype=F32)
    qbd = (q4.transpose(0, 2, 1, 3)[:, :, :, None, :] * eye[None, :, None, :, None]).reshape(b, rows, w)
    qbd = jnp.concatenate(_split2(qbd), axis=1)
    cum_h = cum[:, :, :nh].transpose(0, 2, 1)
    colb = cum_h.reshape(b, rows, 1)
    newb = jnp.pad(jnp.repeat(cum_h, s_new, axis=1), ((0, 0), (0, 0), (0, ps - s_new)))
    padt = lambda a: jnp.pad(a.transpose(0, 2, 1), ((0, 0), (0, 0), (0, ps - s_new)))
    knt = padt(k16)
    vnt = padt(v16)

    def page_spec(r):
        return pl.BlockSpec((None, None, nh, hd, ps), lambda bi, j, pt: (layer, pt[bi, j * pp + r], 0, 0, 0))

    grid_spec = pltpu.PrefetchScalarGridSpec(
        num_scalar_prefetch=1,
        grid=(b, n_pages // pp),
        in_specs=[
            pl.BlockSpec((None, 2 * rows, w), lambda bi, j, pt: (bi, 0, 0)),
            pl.BlockSpec((None, rows, 1), lambda bi, j, pt: (bi, 0, 0)),
            pl.BlockSpec((None, pp, nh, ps), lambda bi, j, pt: (bi, j, 0, 0)),
            pl.BlockSpec((None, w, ps), lambda bi, j, pt: (bi, 0, 0)),
            pl.BlockSpec((None, w, ps), lambda bi, j, pt: (bi, 0, 0)),
            pl.BlockSpec((None, rows, ps), lambda bi, j, pt: (bi, 0, 0)),
        ] + [page_spec(r) for r in range(pp)] + [page_spec(r) for r in range(pp)],
        out_specs=pl.BlockSpec((None, s_new, w), lambda bi, j, pt: (bi, 0, 0)),
        scratch_shapes=[pltpu.VMEM((rows, 1), F32), pltpu.VMEM((rows, 1), F32), pltpu.VMEM((rows, w), F32)],
    )
    return pl.pallas_call(
        functools.partial(_decode_kernel, pp, nh, ps, s_new),
        out_shape=jax.ShapeDtypeStruct((b, s_new, w), ydt),
        grid_spec=grid_spec,
        compiler_params=_cp(("parallel", "arbitrary")),
        name="fox_decode",
    )(page_table, qbd, colb, bias, knt, vnt, newb, *([ckt] * pp), *([cvt] * pp))


def _ret_kernel(c, n_chunks, precise, q_ref, k_ref, v_ref, rg_ref, r0_ref, dm_ref, qd_ref, kd_ref, cd_ref, nw_ref, g_ref,
                y_ref, ro_ref, r_s):
    cp = max(c, LANES)
    lane = lax.broadcasted_iota(jnp.int32, (c, LANES), 1)
    low = lane < HEAD_DIM
    br = _idiv(lax.broadcasted_iota(jnp.int32, (LANES, LANES), 0), HEAD_DIM)
    bc = _idiv(lax.broadcasted_iota(jnp.int32, (LANES, LANES), 1), HEAD_DIM)
    blockdiag = br == bc
    g = g_ref[...]
    r_s[...] = r0_ref[...]

    def gmean(t):
        hi, lo = _split2(t)
        return _dot(hi, g) + _dot(lo, g)

    def chunk(ci, carry):
        r0 = ci * c if isinstance(ci, int) else pl.multiple_of(ci * c, c)
        q = q_ref[pl.ds(r0, c), :]
        k = k_ref[pl.ds(r0, c), :].astype(F32)
        v = v_ref[pl.ds(r0, c), :].astype(F32)
        if c < cp:
            pad = jnp.zeros((cp - c, LANES), F32)
            k = jnp.concatenate([k, pad], axis=0)
            v = jnp.concatenate([v, pad], axis=0)
        if not precise:
            v = v.astype(BF16)
        rmat = r_s[...]
        zero = jnp.zeros_like(q)
        o = _dx(q, rmat, precise) * qd_ref[...]
        for hh in range(2):
            sel = low if hh == 0 else jnp.logical_not(low)
            s = _dx(jnp.where(sel, q, zero), k, precise, nt=True) * dm_ref[hh]
            oh = _dx(s, v, precise)
            o = o + jnp.where(sel, oh, 0.0)
        rn = _dx((k * kd_ref[...]).T, v, precise)
        r_s[...] = rmat * cd_ref[...] + jnp.where(blockdiag, rn, 0.0)
        mu = gmean(o)
        oc = o - mu
        var = gmean(oc * oc)
        on = oc * lax.rsqrt(var + EPS) * nw_ref[...]
        y_ref[pl.ds(r0, c), :] = (on * _silu(rg_ref[pl.ds(r0, c), :])).astype(y_ref.dtype)
        return carry

    if n_chunks == 1:
        chunk(0, 0)
    else:
        lax.fori_loop(0, n_chunks, chunk, 0)
    ro_ref[...] = r_s[...]


def _ret_tables(c):
    cp = max(c, LANES)
    hidx = np.arange(RET_HEADS, dtype=np.float64)
    log_g = np.log1p(-np.exp2(-5.0 - hidx))
    j = np.arange(c, dtype=np.float64)
    diff = j[:, None] - j[None, :]
    dmask = np.where(diff >= 0, np.exp(log_g[:, None, None] * np.maximum(diff, 0.0)), 0.0)
    dm = np.zeros((RET_HEADS, c, cp))
    dm[:, :, :c] = dmask
    q_dec = np.exp(log_g[:, None] * (j + 1.0))
    k_dec = np.exp(log_g[:, None] * (c - 1.0 - j))
    c_dec = np.exp(log_g * c)
    npair = RET_HEADS // 2
    qd = np.zeros((npair, c, LANES))
    kd = np.zeros((npair, cp, LANES))
    cd = np.zeros((npair, 1, LANES))
    for p in range(npair):
        for hh in range(2):
            sl = slice(hh * HEAD_DIM, (hh + 1) * HEAD_DIM)
            qd[p, :, sl] = q_dec[2 * p + hh][:, None]
            kd[p, :c, sl] = k_dec[2 * p + hh][:, None]
            cd[p, 0, sl] = c_dec[2 * p + hh]
    f = lambda a: jnp.asarray(a, F32)
    return f(dm.reshape(npair, 2, c, cp)), f(qd), f(kd), f(cd)


def _retention(rqkv, rg, r0, nw, gpair, c, precise, ydt):
    b, t, _ = rg.shape
    npair = RET_HEADS // 2
    cp = max(c, LANES)
    dm, qd, kd, cd = _ret_tables(c)
    col = lambda off: pl.BlockSpec((None, t, LANES), lambda bi, p: (bi, 0, off + p))
    return pl.pallas_call(
        functools.partial(_ret_kernel, c, t // c, precise),
        out_shape=(jax.ShapeDtypeStruct((b, t, RET_WIDTH), ydt),
                   jax.ShapeDtypeStruct((b, npair, LANES, LANES), F32)),
        grid=(b, npair),
        in_specs=[
            col(0), col(npair), col(2 * npair),
            pl.BlockSpec((None, t, LANES), lambda bi, p: (bi, 0, p)),
            pl.BlockSpec((None, None, LANES, LANES), lambda bi, p: (bi, p, 0, 0)),
            pl.BlockSpec((None, 2, c, cp), lambda bi, p: (p, 0, 0, 0)),
            pl.BlockSpec((None, c, LANES), lambda bi, p: (p, 0, 0)),
            pl.BlockSpec((None, cp, LANES), lambda bi, p: (p, 0, 0)),
            pl.BlockSpec((None, 1, LANES), lambda bi, p: (p, 0, 0)),
            pl.BlockSpec((1, LANES), lambda bi, p: (0, p)),
            pl.BlockSpec((LANES, LANES), lambda bi, p: (0, 0)),
        ],
        out_specs=(pl.BlockSpec((None, t, LANES), lambda bi, p: (bi, 0, p)),
                   pl.BlockSpec((None, None, LANES, LANES), lambda bi, p: (bi, p, 0, 0))),
        scratch_shapes=[pltpu.VMEM((LANES, LANES), F32)],
        compiler_params=_cp(("parallel", "parallel")),
        name="retention",
    )(rqkv, rqkv, rqkv, rg, r0, dm, qd, kd, cd, nw, gpair)


def _route_rows(logits, bias):
    rows = [logits[e:e + 1, :] for e in range(N_EXPERTS)]
    m = rows[0]
    for r in rows[1:]:
        m = jnp.maximum(m, r)
    ex = [jnp.exp(r - m) for r in rows]
    z = ex[0]
    for e in ex[1:]:
        z = z + e
    score = [e / z for e in ex]
    sel = [score[e] + bias[e:e + 1, :] for e in range(N_EXPERTS)]
    top2, gscore = [], []
    for g in range(N_GROUPS):
        v = sel[g * EXPERTS_PER_GROUP:(g + 1) * EXPERTS_PER_GROUP]
        gs = None
        for i in range(EXPERTS_PER_GROUP):
            rank = None
            for j in range(EXPERTS_PER_GROUP):
                if j == i:
                    continue
                ahead = (v[j] >= v[i]) if j < i else (v[j] > v[i])
                ahead = jnp.where(ahead, 1.0, 0.0)
                rank = ahead if rank is None else rank + ahead
            keep = rank < 1.5
            top2.append(keep)
            term = jnp.where(keep, v[i], 0.0)
            gs = term if gs is None else gs + term
        gscore.append(gs)
    out = []
    for g in range(N_GROUPS):
        win = None
        for j in range(N_GROUPS):
            if j == g:
                continue
            c = (gscore[g] > gscore[j]) if j < g else (gscore[g] >= gscore[j])
            win = c if win is None else (win & c)
        for i in range(EXPERTS_PER_GROUP):
            e = g * EXPERTS_PER_GROUP + i
            out.append(jnp.where(win & top2[e], score[e], 0.0))
    tot = out[0]
    for o in out[1:]:
        tot = tot + o
    return jnp.concatenate([o / tot for o in out], axis=0)


def _out_kernel(precise, ya_ref, yb_ref, yc_ref, x_ref, g1_ref, sh_ref, sc_ref, n2_ref, rwh_ref, rwl_ref, rb_ref,
                *rest):
    nw = 2 if precise else 1
    w_refs, (x1_ref, h2_ref, cmb_ref) = rest[:nw], rest[nw:]
    mix = _mm(_lhs(ya_ref[...], precise), w_refs, (slice(0, 256), slice(None)))
    mix = mix + _mm(_lhs(yb_ref[...], precise), w_refs, (slice(256, 768), slice(None)))
    mix = mix + _mm(_lhs(yc_ref[...], precise), w_refs, (slice(768, 1024), slice(None)))
    x1 = x_ref[...] + g1_ref[...] * mix
    x1_ref[...] = x1
    h2 = _rms(x1, n2_ref[...]) * (1.0 + sc_ref[...]) + sh_ref[...]
    hi, lo = _split2(h2)
    h2_ref[...] = h2
    rwh = rwh_ref[...]
    logits = _dot_nt(rwh, hi) + (_dot_nt(rwh, lo) + _dot_nt(rwl_ref[...], hi))
    cmb_ref[...] = _route_rows(logits, rb_ref[...])


def _out_proj(layer, ya, yb, yc, x, mod, per_token, w_out_hl, n2, rwh, rwl, rb, tm, precise):
    b, t, d = x.shape
    tmm = tm if per_token else 1
    mi = (lambda bi, ti: ti) if per_token else (lambda bi, ti: 0)
    tok = lambda w: pl.BlockSpec((None, tm, w), lambda bi, ti: (bi, ti, 0))
    modspec = lambda col: pl.BlockSpec((None, tmm, d), lambda bi, ti: (bi, mi(bi, ti), col))
    w_list = list(w_out_hl) if precise else [w_out_hl[0]]
    return pl.pallas_call(
        functools.partial(_out_kernel, precise),
        out_shape=(jax.ShapeDtypeStruct((b, t, d), F32), jax.ShapeDtypeStruct((b, t, d), F32),
                   jax.ShapeDtypeStruct((b, N_EXPERTS, t), F32)),
        grid=(b, t // tm),
        in_specs=[
            tok(LRU_WIDTH), tok(FOX_WIDTH), tok(RET_WIDTH), tok(d),
            modspec(2), modspec(3), modspec(4),
            pl.BlockSpec((None, 1, d), lambda bi, ti: (layer, 0, 0)),
            pl.BlockSpec((N_EXPERTS, d), lambda bi, ti: (0, 0)),
            pl.BlockSpec((N_EXPERTS, d), lambda bi, ti: (0, 0)),
            pl.BlockSpec((N_EXPERTS, 1), lambda bi, ti: (0, 0)),
        ] + [pl.BlockSpec((None, d, d), lambda bi, ti: (layer, 0, 0)) for _ in w_list],
        out_specs=(tok(d), tok(d), pl.BlockSpec((None, N_EXPERTS, tm), lambda bi, ti: (bi, 0, ti))),
        compiler_params=_cp(("parallel", "parallel")),
        name="out_proj_route",
    )(ya, yb, yc, x, mod, mod, mod, n2, rwh, rwl, rb, *w_list)


def _moe_kernel(h_ref, c_ref, x1_ref, g2_ref, wg_ref, wu_ref, wd_ref, o_ref, acc):
    e = pl.program_id(2)

    @pl.when(e == 0)
    def _():
        acc[...] = jnp.zeros_like(acc)

    h = h_ref[...].astype(BF16)
    act = _silu(_dot(h, wg_ref[...])) * _dot(h, wu_ref[...])
    cmb = c_ref[...]
    lane = lax.broadcasted_iota(jnp.int32, cmb.shape, 1)
    ce = jnp.sum(jnp.where(lane == e, cmb, 0.0), axis=-1, keepdims=True)
    acc[...] += _dot((ce * act).astype(BF16), wd_ref[...])

    @pl.when(e == pl.num_programs(2) - 1)
    def _():
        o_ref[...] = x1_ref[...] + g2_ref[...] * acc[...]


def _moe(layer, h2, cmb, x1, mod, per_token, wg, wu, wd, tm):
    b, t, d = x1.shape
    de = wg.shape[-1]
    tmm = tm if per_token else 1
    mi = (lambda bi, ti: ti) if per_token else (lambda bi, ti: 0)
    tok = lambda w: pl.BlockSpec((None, tm, w), lambda bi, ti, e: (bi, ti, 0))
    return pl.pallas_call(
        _moe_kernel,
        out_shape=jax.ShapeDtypeStruct((b, t, d), F32),
        grid=(b, t // tm, N_EXPERTS),
        in_specs=[
            tok(d), tok(N_EXPERTS), tok(d),
            pl.BlockSpec((None, tmm, d), lambda bi, ti, e: (bi, mi(bi, ti), 5)),
            pl.BlockSpec((None, None, d, de), lambda bi, ti, e: (layer, e, 0, 0)),
            pl.BlockSpec((None, None, d, de), lambda bi, ti, e: (layer, e, 0, 0)),
            pl.BlockSpec((None, None, de, d), lambda bi, ti, e: (layer, e, 0, 0)),
        ],
        out_specs=tok(d),
        scratch_shapes=[pltpu.VMEM((tm, d), F32)],
        compiler_params=_cp(("parallel", "parallel", "arbitrary")),
        name="moe_dense",
    )(h2, cmb, x1, mod, wg, wu, wd)


def _plan_kernel(tm, cmb_ref, off_ref, pos_ref, w_ref, base):
    i = pl.program_id(0)

    @pl.when(i == 0)
    def _():
        base[...] = jnp.zeros_like(base)

    cmb = cmb_ref[...]
    sel = cmb > 0.0
    r = lax.broadcasted_iota(jnp.int32, (tm, tm), 0)
    c = lax.broadcasted_iota(jnp.int32, (tm, tm), 1)
    upto = jnp.where(r <= c, 1.0, 0.0).astype(BF16)
    incl = _dot(jnp.where(sel, 1.0, 0.0).astype(BF16), upto)
    pos = incl - 1.0 + base[...] + off_ref[...]
    base[...] = base[...] + incl[:, tm - 1:tm]
    p0 = jnp.min(jnp.where(sel, pos, 3.0e8), axis=0, keepdims=True)
    p1 = jnp.max(jnp.where(sel, pos, -1.0), axis=0, keepdims=True)
    w0 = jnp.sum(jnp.where(sel & (pos == p0), cmb, 0.0), axis=0, keepdims=True)
    w1 = jnp.sum(jnp.where(sel & (pos == p1), cmb, 0.0), axis=0, keepdims=True)
    w1 = jnp.where(p1 == p0, 0.0, w1)
    pos_ref[0:1, :] = p0.astype(jnp.int32)
    pos_ref[1:2, :] = p1.astype(jnp.int32)
    w_ref[0:1, :] = w0
    w_ref[1:2, :] = w1


def _plan(cmb_t, offsets, tm):
    b, e, t = cmb_t.shape
    nt = t // tm
    return pl.pallas_call(
        functools.partial(_plan_kernel, tm),
        out_shape=(jax.ShapeDtypeStruct((2, b * t), jnp.int32), jax.ShapeDtypeStruct((2, b * t), F32)),
        grid=(b * nt,),
        in_specs=[pl.BlockSpec((None, e, tm), lambda i: (i // nt, 0, i % nt)),
                  pl.BlockSpec((e, 1), lambda i: (0, 0))],
        out_specs=(pl.BlockSpec((2, tm), lambda i: (0, i)), pl.BlockSpec((2, tm), lambda i: (0, i))),
        scratch_shapes=[pltpu.VMEM((e, 1), F32)],
        compiler_params=_cp(("arbitrary",)),
        name="moe_plan",
    )(cmb_t, offsets)


def _dispatch_kernel(tm, pos_ref, h_ref, xs_in, xs_out, sem):
    del xs_in

    def copy(t, slot):
        return pltpu.make_async_copy(h_ref.at[pl.ds(t, 1)], xs_out.at[pl.ds(pos_ref[0, slot * tm + t], 1)], sem)

    def start(t, c):
        copy(t, 0).start(priority=0)
        copy(t, 1).start(priority=1)
        return c

    def wait(t, c):
        copy(t, 0).wait()
        copy(t, 1).wait()
        return c

    lax.fori_loop(0, tm, start, 0)
    lax.fori_loop(0, tm, wait, 0)


def _dispatch(h2, pos_tiles, n_rows, tm):
    b, t, d = h2.shape
    nt = t // tm
    return pl.pallas_call(
        functools.partial(_dispatch_kernel, tm),
        out_shape=jax.ShapeDtypeStruct((n_rows, d), h2.dtype),
        grid=(b, nt),
        in_specs=[pl.BlockSpec((None, 1, 2 * tm), lambda bi, ti: (bi * nt + ti, 0, 0), memory_space=pltpu.SMEM),
                  pl.BlockSpec((tm, d), lambda bi, ti: (bi * nt + ti, 0)),
                  pl.BlockSpec(memory_space=pl.ANY)],
        out_specs=pl.BlockSpec(memory_space=pl.ANY),
        scratch_shapes=[pltpu.SemaphoreType.DMA(())],
        input_output_aliases={2: 0},
        compiler_params=_cp(("arbitrary", "arbitrary")),
        name="moe_dispatch",
    )(pos_tiles, h2.reshape(b * t, d), jnp.zeros((n_rows, d), h2.dtype))


def _gmm_kernel(te_ref, nv_ref, x_ref, wg_ref, wu_ref, wd_ref, y_ref):
    i = pl.program_id(0)

    @pl.when(i < nv_ref[0])
    def _():
        x = x_ref[...].astype(BF16)
        act = _silu(_dot(x, wg_ref[...])) * _dot(x, wu_ref[...])
        y_ref[...] = _dot(act.astype(BF16), wd_ref[...])

    @pl.when(i >= nv_ref[0])
    def _():
        y_ref[...] = jnp.zeros_like(y_ref)


def _gmm(layer, xs, tile_expert, n_valid, wg, wu, wd, tr):
    n_rows, d = xs.shape
    de = wg.shape[-1]
    grid_spec = pltpu.PrefetchScalarGridSpec(
        num_scalar_prefetch=2,
        grid=(n_rows // tr,),
        in_specs=[
            pl.BlockSpec((tr, d), lambda i, te, nv: (i, 0)),
            pl.BlockSpec((None, None, d, de), lambda i, te, nv: (layer, te[i], 0, 0)),
            pl.BlockSpec((None, None, d, de), lambda i, te, nv: (layer, te[i], 0, 0)),
            pl.BlockSpec((None, None, de, d), lambda i, te, nv: (layer, te[i], 0, 0)),
        ],
        out_specs=pl.BlockSpec((tr, d), lambda i, te, nv: (i, 0)),
    )
    return pl.pallas_call(
        _gmm_kernel,
        out_shape=jax.ShapeDtypeStruct((n_rows, d), F32),
        grid_spec=grid_spec,
        compiler_params=_cp(("arbitrary",)),
        name="moe_experts",
    )(tile_expert, n_valid, xs, wg, wu, wd)


def _combine_kernel(tm, pos_ref, w_ref, x1_ref, g2_ref, ys_hbm, o_ref, buf0, buf1, sem):
    buf = (buf0, buf1)

    def copy(t, slot):
        return pltpu.make_async_copy(ys_hbm.at[pl.ds(pos_ref[0, slot * tm + t], 1)], buf[slot].at[pl.ds(t, 1)], sem)

    def start(t, c):
        copy(t, 0).start(priority=0)
        copy(t, 1).start(priority=1)
        return c

    def wait(t, c):
        copy(t, 0).wait()
        copy(t, 1).wait()
        return c

    lax.fori_loop(0, tm, start, 0)
    lax.fori_loop(0, tm, wait, 0)
    w = w_ref[...]
    y = w[:, 0:1] * buf0[...] + w[:, 1:2] * buf1[...]
    o_ref[...] = x1_ref[...] + g2_ref[...] * y


def _combine(ys, pos_tiles, w_tok, x1, mod, tm):
    b, t, d = x1.shape
    nt = t // tm
    return pl.pallas_call(
        functools.partial(_combine_kernel, tm),
        out_shape=jax.ShapeDtypeStruct((b, t, d), F32),
        grid=(b, nt),
        in_specs=[pl.BlockSpec((None, 1, 2 * tm), lambda bi, ti: (bi * nt + ti, 0, 0), memory_space=pltpu.SMEM),
                  pl.BlockSpec((tm, 2), lambda bi, ti: (bi * nt + ti, 0)),
                  pl.BlockSpec((None, tm, d), lambda bi, ti: (bi, ti, 0)),
                  pl.BlockSpec((None, 1, d), lambda bi, ti: (bi, 0, 5)),
                  pl.BlockSpec(memory_space=pl.ANY)],
        out_specs=pl.BlockSpec((None, tm, d), lambda bi, ti: (bi, ti, 0)),
        scratch_shapes=[pltpu.VMEM((tm, d), F32), pltpu.VMEM((tm, d), F32), pltpu.SemaphoreType.DMA(())],
        compiler_params=_cp(("arbitrary", "arbitrary")),
        name="moe_combine",
    )(pos_tiles, w_tok, x1, mod, ys)


def _moe_sorted(layer, h2, cmb_t, x1, mod, wg, wu, wd, tm, tr):
    b, t, d = x1.shape
    n = b * t
    e = cmb_t.shape[1]
    n_rows = 2 * n + e * tr
    n_tiles = n_rows // tr
    counts = jnp.sum(cmb_t > 0.0, axis=(0, 2)).astype(jnp.int32)
    padded = (counts + tr - 1) // tr * tr
    ends = jnp.cumsum(padded)
    offsets = (ends - padded).astype(F32).reshape(e, 1)
    tile_expert = jnp.minimum(jnp.sum(ends[None, :] <= (jnp.arange(n_tiles, dtype=jnp.int32) * tr)[:, None], axis=1),
                              e - 1).astype(jnp.int32)
    n_valid = (ends[-1:] // tr).astype(jnp.int32)
    pos, w = _plan(cmb_t, offsets, min(t, 512))
    pos_tiles = pos.reshape(2, n // tm, tm).transpose(1, 0, 2).reshape(n // tm, 1, 2 * tm)
    xs = _dispatch(h2, pos_tiles, n_rows, tm)
    ys = _gmm(layer, xs, tile_expert, n_valid, wg, wu, wd, tr)
    return _combine(ys, pos_tiles, w.T, x1, mod, tm)


def _rope_tables(pos):
    half = HEAD_DIM // 2
    inv = np.exp(-np.log(ROPE_BASE) * np.arange(half, dtype=np.float64) / half)
    ang = np.asarray(pos, np.float64)[:, None] * inv[None, :]
    reps = RET_WIDTH // half
    return (jnp.asarray(np.tile(np.cos(ang), (1, reps)), F32), jnp.asarray(np.tile(np.sin(ang), (1, reps)), F32))


def _block_diag(w):
    depth, nb, h, _ = w.shape
    eye = jnp.eye(nb, dtype=w.dtype)
    return (w[:, :, :, None, :] * eye[None, :, None, :, None]).reshape(depth, nb * h, nb * h)


def _group_mean_matrix(width):
    idx = np.arange(width) // HEAD_DIM
    return jnp.asarray((idx[:, None] == idx[None, :]) / HEAD_DIM, BF16)


def _trunk(x, mod, per_token, seq_shape, pos, lru_h0, lru_buf0, ret_r0, ret_chunk, attend, prm, tm, tm_moe, precise,
           log2_scores):
    bv, tv, d = x.shape
    b, t = seq_shape
    depth = prm['w_in_hl'][0].shape[0]
    cos_t, sin_t = _rope_tables(pos)
    kv = None
    logfs, lru_hs, lru_bufs, rets = [], [], [], []
    npair = RET_HEADS // 2
    for l in range(depth):
        m = mod[l]
        xbgb, q16, k32, k16, v32, v16, rqkv, rg, lf = _proj(
            l, x, m, per_token, prm['norm1_w'], prm['w_in_hl'], prm['q_norm_w'], prm['k_norm_w'], prm['forget_bias'],
            prm['g512'], cos_t, sin_t, kv, tm, precise or l < depth - 1, F32 if precise else BF16,
            HEAD_DIM ** -0.5 * (LOG2E if log2_scores else 1.0))
        kv = (k32, v32)
        seq = lambda a: a.reshape(b, t, a.shape[-1])
        pick = lambda hl: tuple(w[l] for w in hl)
        ya, lru_h, lru_buf = _lru(seq(xbgb), lru_h0[l], lru_buf0[l], prm['conv_w'][l], prm['conv_b'][l],
                                  pick(prm['lru_wa_hl']), prm['lru_ba'][l], pick(prm['lru_wx_hl']), prm['lru_bx'][l],
                                  prm['lru_lambda'][l], precise, F32)
        lf_s = seq(lf)
        cum, cumt = _cum(lf_s, LOG2E if log2_scores else 1.0)
        yb = attend(l, seq(q16), seq(k16), seq(v16), cum, cumt)
        yc, r_out = _retention(seq(rqkv), seq(rg), ret_r0[l], prm['ret_norm_w'][l], prm['g128'], ret_chunk, precise, F32)
        tokv = lambda a: a.reshape(bv, tv, a.shape[-1])
        x1, h2, cmb_t = _out_proj(l, tokv(ya), tokv(yb), tokv(yc), x, m, per_token, prm['w_out_hl'], prm['norm2_w'],
                                  prm['rw_hi'], prm['rw_lo'], prm['router_bias'], tm, precise or l < depth - 1)
        if per_token:
            x = _moe(l, h2, cmb_t.transpose(0, 2, 1), x1, m, per_token, prm['w_gate16'], prm['w_up16'],
                     prm['w_down16'], tm_moe)
        else:
            x = _moe_sorted(l, h2, cmb_t, x1, m, prm['w_gate16'], prm['w_up16'], prm['w_down16'], min(t, 256), tm_moe)
        logfs.append(lf_s[:, :, :FOX_HEADS])
        lru_hs.append(lru_h[:, 0, :])
        lru_bufs.append(lru_buf)
        r4 = r_out.reshape(b, npair, 2, HEAD_DIM, 2, HEAD_DIM)
        rets.append(jnp.stack([r4[:, :, 0, :, 0, :], r4[:, :, 1, :, 1, :]], axis=2).reshape(
            b, RET_HEADS, HEAD_DIM, HEAD_DIM))
    k_out = kv[0].reshape(depth, b, t, FOX_HEADS, HEAD_DIM)
    v_out = kv[1].reshape(depth, b, t, FOX_HEADS, HEAD_DIM)
    return x, (k_out, v_out, jnp.stack(logfs), jnp.stack(lru_hs), jnp.stack(lru_bufs), jnp.stack(rets))


def _pair_state(r):
    depth, b, h, d, _ = r.shape
    r6 = r.reshape(depth, b, h // 2, 2, d, d)
    eye = jnp.eye(2, dtype=r.dtype)
    out = r6[:, :, :, :, :, None, :] * eye[None, None, None, :, None, :, None]
    return out.reshape(depth, b, h // 2, 2 * d, 2 * d)


def kernel(x_prompt, x_sample, cache_k, cache_v, cache_logf, state_lru_h, state_lru_conv, state_ret, page_table,
           c_prompt, c_sample, w_in, w_out, conv_w, conv_b, lru_wa, lru_ba, lru_wx, lru_bx, lru_lambda, q_norm_w,
           k_norm_w, forget_bias, ret_norm_w, norm1_w, norm2_w, w_ada, b_ada, router_w, router_bias, w_gate, w_up,
           w_down):
    depth, d, _ = w_in.shape
    bp = x_prompt.shape[0]
    bs, ts, _ = x_sample.shape
    prm = _prep(w_in=w_in, w_out=w_out, conv_w=conv_w, conv_b=conv_b, lru_wa=lru_wa, lru_ba=lru_ba, lru_wx=lru_wx,
                lru_bx=lru_bx, lru_lambda=lru_lambda, q_norm_w=q_norm_w, k_norm_w=k_norm_w, forget_bias=forget_bias,
                ret_norm_w=ret_norm_w, norm1_w=norm1_w, norm2_w=norm2_w, router_w=router_w, router_bias=router_bias,
                w_gate=w_gate, w_up=w_up, w_down=w_down)

    mod = _ada(jnp.concatenate([c_prompt, c_sample], axis=0), w_ada, b_ada)
    mod_p = mod[:, :bp].reshape(depth, bp, 1, 6 * d)
    mod_s = jnp.repeat(mod[:, bp:], ts, axis=1).reshape(depth, 1, bs * ts, 6 * d)
    return _run(prm, mod_p, mod_s, x_prompt, x_sample, cache_k, cache_v, cache_logf, state_lru_h, state_lru_conv,
                state_ret, page_table)


def _prep(w_in, w_out, conv_w, conv_b, lru_wa, lru_ba, lru_wx, lru_bx, lru_lambda, q_norm_w, k_norm_w, forget_bias,
          ret_norm_w, norm1_w, norm2_w, router_w, router_bias, w_gate, w_up, w_down, **unused):
    depth, d, n_in = w_in.shape
    ff0 = 2 * LRU_WIDTH + 3 * FOX_WIDTH
    w_in_p = jnp.concatenate(
        [w_in[:, :, :ff0], w_in[:, :, ff0 + FOX_HEADS:], w_in[:, :, ff0:ff0 + FOX_HEADS],
         jnp.zeros((depth, d, N_PROJ - n_in), w_in.dtype)], axis=-1)
    rw_hi, rw_lo = _split2(router_w.T)
    return dict(
        w_in_hl=_split2(w_in_p),
        norm1_w=norm1_w.reshape(depth, 1, d),
        norm2_w=norm2_w.reshape(depth, 1, d),
        q_norm_w=jnp.tile(q_norm_w, (1, FOX_HEADS)).reshape(depth, 1, FOX_WIDTH),
        k_norm_w=jnp.tile(k_norm_w, (1, FOX_HEADS)).reshape(depth, 1, FOX_WIDTH),
        forget_bias=jnp.pad(forget_bias, ((0, 0), (0, LANES - FOX_HEADS))).reshape(depth, 1, LANES),
        g512=_group_mean_matrix(FOX_WIDTH),
        g128=_group_mean_matrix(LANES),
        conv_w=conv_w,
        conv_b=conv_b.reshape(depth, 1, LRU_WIDTH),
        lru_wa_hl=_split2(_block_diag(lru_wa)),
        lru_wx_hl=_split2(_block_diag(lru_wx)),
        lru_ba=lru_ba.reshape(depth, 1, LRU_WIDTH),
        lru_bx=lru_bx.reshape(depth, 1, LRU_WIDTH),
        lru_lambda=lru_lambda.reshape(depth, 1, LRU_WIDTH),
        ret_norm_w=ret_norm_w.reshape(depth, 1, RET_WIDTH),
        w_out_hl=_split2(w_out),
        rw_hi=rw_hi, rw_lo=rw_lo,
        router_bias=router_bias.reshape(N_EXPERTS, 1),
        w_gate16=w_gate.astype(BF16), w_up16=w_up.astype(BF16), w_down16=w_down.astype(BF16),
    )


def _run(prm, mod_p, mod_s, x_prompt, x_sample, cache_k, cache_v, cache_logf, state_lru_h, state_lru_conv, state_ret,
         page_table):
    depth = prm['w_in_hl'][0].shape[0]
    bp, tp, d = x_prompt.shape
    bs, ts, _ = x_sample.shape
    n_pages = page_table.shape[1]
    past = n_pages * cache_k.shape[2]

    npair = RET_HEADS // 2
    zeros = lambda *s: jnp.zeros(s, F32)
    tq = min(tp, 512)
    y_prompt, outs_p = _trunk(
        x_prompt, mod_p, False, (bp, tp), np.arange(tp),
        zeros(depth, bp, 1, LRU_WIDTH), zeros(depth, bp, CONV_W - 1, LRU_WIDTH),
        zeros(depth, bp, npair, LANES, LANES), min(tp, 128),
        lambda l, q, k, v, cum, cumt: _fox_prompt(q, k, v, cumt, tq, F32),
        prm, min(tp, 512), min(tp, 512), False, True)

    pp = 8 if n_pages % 8 == 0 else 1

    def attend_sample(l, q, k, v, cum, cumt):
        bias = _suffix_bias(l, cache_logf, page_table)
        return _fox_decode(l, q, k, v, cum, cache_k, cache_v, bias, page_table, pp, F32)

    n_s = bs * ts
    y_s, outs_s = _trunk(
        x_sample.reshape(1, n_s, d), mod_s, True, (bs, ts), past + np.tile(np.arange(ts), bs),
        state_lru_h.reshape(depth, bs, 1, LRU_WIDTH), state_lru_conv, _pair_state(state_ret), ts,
        attend_sample, prm, n_s, n_s, True, False)
    return (y_prompt, y_s.reshape(bs, ts, d)) + tuple(outs_p) + tuple(outs_s)
```
